```python
import math
import numpy as np
import jax
import jax.numpy as jnp
from jax import lax

D_MODEL = 2048
BATCH = 16
SEQ = 2048
DEPTH = 1

SSD_D_INNER = D_MODEL
SSD_HEAD_DIM = 64
SSD_HEADS = SSD_D_INNER // SSD_HEAD_DIM
SSD_GROUPS = 4
SSD_D_STATE = 128
SSD_CONV_K = 4
SSD_CHUNK = 128
SSD_CONV_DIM = SSD_D_INNER + 2 * SSD_GROUPS * SSD_D_STATE

NSA_HEADS = 16
NSA_HEAD_DIM = 128
NSA_KV_GROUPS = 2
NSA_HPG = NSA_HEADS // NSA_KV_GROUPS
NSA_D = NSA_HEADS * NSA_HEAD_DIM
NSA_KV_D = NSA_KV_GROUPS * NSA_HEAD_DIM
N_NSA_BRANCHES = 3
CMP_BLOCK = 32
CMP_STRIDE = 16
SEL_BLOCK = 64
SEL_TOPK = 8
WINDOW = 512
Q_BLOCK = 128

IN_DIM = (SSD_D_INNER + SSD_CONV_DIM + SSD_HEADS
          + NSA_D + 6 * NSA_KV_D + NSA_D + NSA_HEADS * N_NSA_BRANCHES
          + 2 * D_MODEL)
EPS = 1e-6
NEG_INF = -1e30
FORCED_SCORE = 1e9

kernel_name = 'hybrid_ssd_nsa_gated_block'


def rms_norm(x, w):
    xf = x.astype(jnp.float32)
    xf = xf * lax.rsqrt(jnp.mean(xf * xf, axis=-1, keepdims=True) + EPS)
    return (xf * w.astype(jnp.float32)).astype(x.dtype)


def masked_softmax(s, mask):
    s = jnp.where(mask, s.astype(jnp.float32), NEG_INF)
    return jnp.where(mask, jax.nn.softmax(s, axis=-1), 0.0)


def causal_dwconv(u, w, b):
    k, c = w.shape
    y = lax.conv_general_dilated(u, w[:, None, :].astype(u.dtype), window_strides=(1,),
                                 padding=[(k - 1, 0)], dimension_numbers=('NWC', 'WIO', 'NWC'),
                                 feature_group_count=c)
    return y + b.astype(u.dtype)


def ssd_chunked(x, dt, a, bmat, cmat):
    b_, t_, h_, p_ = x.shape
    g_, n_ = bmat.shape[2], bmat.shape[3]
    hpg = h_ // g_
    q_ = SSD_CHUNK
    nc = t_ // q_
    xc = (x * dt[..., None].astype(x.dtype)).reshape(b_, nc, q_, g_, hpg, p_)
    bc = bmat.reshape(b_, nc, q_, g_, n_)
    cc = cmat.reshape(b_, nc, q_, g_, n_)
    la = (dt * a).reshape(b_, nc, q_, g_, hpg).transpose(0, 1, 3, 4, 2)
    la_cum = jnp.cumsum(la, axis=-1)
    causal = jnp.tril(jnp.ones((q_, q_), bool))
    seg = la_cum[..., :, None] - la_cum[..., None, :]
    lmat = jnp.exp(jnp.where(causal, seg, -jnp.inf)).astype(x.dtype)
    cb = jnp.einsum('bclgn,bcsgn->bcgls', cc, bc)
    y_diag = jnp.einsum('bcgls,bcgkls,bcsgkp->bclgkp', cb, lmat, xc)
    decay_to_end = jnp.exp(la_cum[..., -1:] - la_cum).astype(x.dtype)
    states = jnp.einsum('bcsgn,bcgks,bcsgkp->bcgkpn', bc, decay_to_end, xc)
    chunk_decay = jnp.exp(la_cum[..., -1])

    def step(h, inp):
        st, dec = inp
        return h * dec[..., None, None] + st.astype(jnp.float32), h

    h0 = jnp.zeros((b_, g_, hpg, p_, n_), jnp.float32)
    _, prev = lax.scan(step, h0, (jnp.moveaxis(states, 1, 0), jnp.moveaxis(chunk_decay, 1, 0)))
    prev = jnp.moveaxis(prev, 0, 1).astype(x.dtype)
    decay_from_start = jnp.exp(la_cum).astype(x.dtype)
    y_off = jnp.einsum('bclgn,bcgkpn,bcgkl->bclgkp', cc, prev, decay_from_start)
    return (y_diag + y_off).reshape(b_, t_, h_, p_)


def ssd_branch(z, xbc, dt_raw, conv_w, conv_b, dt_bias, a_log, d_skip, ssd_norm_w, w_out):
    b, t, _ = xbc.shape
    xbc = jax.nn.silu(causal_dwconv(xbc, conv_w, conv_b))
    xs, bs, cs = jnp.split(xbc, [SSD_D_INNER, SSD_D_INNER + SSD_GROUPS * SSD_D_STATE], axis=-1)
    xs = xs.reshape(b, t, SSD_HEADS, SSD_HEAD_DIM)
    dt = jax.nn.softplus(dt_raw.astype(jnp.float32) + dt_bias.astype(jnp.float32))
    a = -jnp.exp(a_log.astype(jnp.float32))
    y = ssd_chunked(xs, dt, a, bs.reshape(b, t, SSD_GROUPS, SSD_D_STATE),
                    cs.reshape(b, t, SSD_GROUPS, SSD_D_STATE))
    y = y + xs * d_skip[:, None].astype(xs.dtype)
    y = rms_norm(y.reshape(b, t, SSD_D_INNER) * jax.nn.silu(z), ssd_norm_w)
    return y @ w_out


def compress_blocks(kv, pe, w1, b1, w2):
    b, t, g, d = kv.shape
    n_cmp = (t - CMP_BLOCK) // CMP_STRIDE + 1
    idx = np.arange(n_cmp)[:, None] * CMP_STRIDE + np.arange(CMP_BLOCK)[None, :]
    blk = kv[:, idx] + pe[None, None, :, None, :]
    blk = blk.transpose(0, 1, 3, 2, 4).reshape(b, n_cmp, g, CMP_BLOCK * d)
    return jax.nn.silu(blk @ w1 + b1) @ w2


def nsa_branch(q, k_cmp, v_cmp, k_slc, v_slc, k_win, v_win, z, gate_logits,
               q_norm_w, k_cmp_norm_w, k_slc_norm_w, k_win_norm_w,
               cmp_pe_k, cmp_w1_k, cmp_b1_k, cmp_w2_k,
               cmp_pe_v, cmp_w1_v, cmp_b1_v, cmp_w2_v, w_out):
    b, t, _ = q.shape
    dtype = q.dtype
    G, K, HD = NSA_KV_GROUPS, NSA_HPG, NSA_HEAD_DIM
    scale = HD ** -0.5
    q = rms_norm(q.reshape(b, t, NSA_HEADS, HD), q_norm_w).reshape(b, t, G, K, HD)
    kv_shape = (b, t, G, HD)
    k_slc = rms_norm(k_slc.reshape(kv_shape), k_slc_norm_w)
    k_win = rms_norm(k_win.reshape(kv_shape), k_win_norm_w)
    v_slc = v_slc.reshape(kv_shape)
    v_win = v_win.reshape(kv_shape)
    kc = rms_norm(compress_blocks(k_cmp.reshape(kv_shape), cmp_pe_k, cmp_w1_k, cmp_b1_k, cmp_w2_k), k_cmp_norm_w)
    vc = compress_blocks(v_cmp.reshape(kv_shape), cmp_pe_v, cmp_w1_v, cmp_b1_v, cmp_w2_v)
    n_cmp = kc.shape[1]
    cmp_end = jnp.asarray(np.arange(n_cmp) * CMP_STRIDE + CMP_BLOCK - 1, jnp.int32)
    n_slc = t // SEL_BLOCK
    n_sel = min(SEL_TOPK, n_slc)
    ci = np.arange(n_cmp)[:, None] * CMP_STRIDE
    sj = np.arange(n_slc)[None, :] * SEL_BLOCK
    sel_map = jnp.asarray(((ci < sj + SEL_BLOCK) & (ci + CMP_BLOCK > sj)).astype(np.float32))
    ks_blocks = k_slc.reshape(b, n_slc, SEL_BLOCK, G, HD).transpose(0, 3, 1, 2, 4)
    vs_blocks = v_slc.reshape(b, n_slc, SEL_BLOCK, G, HD).transpose(0, 3, 1, 2, 4)
    blk_idx = jnp.arange(n_slc)
    b_idx = jnp.arange(b)[:, None, None, None]
    g_idx = jnp.arange(G)[None, :, None, None]
    pad = ((0, 0), (WINDOW, 0), (0, 0), (0, 0))
    k_win_p = jnp.pad(k_win, pad)
    v_win_p = jnp.pad(v_win, pad)
    gates = jax.nn.sigmoid(gate_logits.astype(jnp.float32)).astype(dtype).reshape(b, t, G, K, N_NSA_BRANCHES)

    def query_block(i):
        qs = i * Q_BLOCK
        tq = qs + jnp.arange(Q_BLOCK)
        qb = lax.dynamic_slice_in_dim(q, qs, Q_BLOCK, axis=1)
        s = jnp.einsum('bqgkd,bcgd->bgkqc', qb, kc) * scale
        p_cmp = masked_softmax(s, cmp_end[None, :] <= tq[:, None])
        o_cmp = jnp.einsum('bgkqc,bcgd->bqgkd', p_cmp.astype(dtype), vc)
        imp = jnp.einsum('bgkqc,cj->bgqj', p_cmp, sel_map)
        blk_valid = blk_idx[None, :] * SEL_BLOCK <= tq[:, None]
        forced = (blk_idx[None, :] == (tq // SEL_BLOCK)[:, None]) | (blk_idx[None, :] == 0)
        imp = jnp.where(forced, FORCED_SCORE, jnp.where(blk_valid, imp, -1.0))
        _, sel = lax.top_k(imp, n_sel)
        kg = ks_blocks[b_idx, g_idx, sel]
        vg = vs_blocks[b_idx, g_idx, sel]
        pos = sel[..., None] * SEL_BLOCK + jnp.arange(SEL_BLOCK)
        mask_s = (pos <= tq[None, None, :, None, None]).reshape(b, G, 1, Q_BLOCK, n_sel * SEL_BLOCK)
        s = jnp.einsum('bqgkd,bgqnld->bgkqnl', qb, kg).reshape(b, G, K, Q_BLOCK, n_sel * SEL_BLOCK) * scale
        p = masked_softmax(s, mask_s).astype(dtype).reshape(b, G, K, Q_BLOCK, n_sel, SEL_BLOCK)
        o_slc = jnp.einsum('bgkqnl,bgqnld->bqgkd', p, vg)
        kw = lax.dynamic_slice_in_dim(k_win_p, qs, Q_BLOCK + WINDOW, axis=1)
        vw = lax.dynamic_slice_in_dim(v_win_p, qs, Q_BLOCK + WINDOW, axis=1)
        kpos = qs - WINDOW + jnp.arange(Q_BLOCK + WINDOW)
        mask_w = ((kpos[None, :] <= tq[:, None]) & (kpos[None, :] > tq[:, None] - WINDOW)
                  & (kpos[None, :] >= 0))
        s = jnp.einsum('bqgkd,bsgd->bgkqs', qb, kw) * scale
        p = masked_softmax(s, mask_w).astype(dtype)
        o_win = jnp.einsum('bgkqs,bsgd->bqgkd', p, vw)
        gb = lax.dynamic_slice_in_dim(gates, qs, Q_BLOCK, axis=1)
        o = gb[..., 0:1] * o_cmp + gb[..., 1:2] * o_slc + gb[..., 2:3] * o_win
        return o.reshape(b, Q_BLOCK, NSA_D)

    out = lax.map(query_block, jnp.arange(t // Q_BLOCK))
    out = jnp.moveaxis(out, 0, 1).reshape(b, t, NSA_D)
    return (out * jax.nn.silu(z)) @ w_out


def setup_inputs(seed: int = 0) -> dict:
    key = jax.random.key(seed)
    ks = jax.random.split(key, 24)
    f32 = jnp.float32
    hd = NSA_HEAD_DIM

    def nrm(k, shape, scale):
        return jax.random.normal(k, shape, f32) * scale

    def gain(k, n):
        return 1.0 + 0.05 * jax.random.normal(k, (n,), f32)

    dt0 = jnp.exp(jax.random.uniform(ks[5], (SSD_HEADS,), f32) * (math.log(0.1) - math.log(0.001))
                  + math.log(0.001))
    dt_bias = dt0 + jnp.log(-jnp.expm1(-dt0))
    return {
        'x': nrm(ks[0], (BATCH, SEQ, D_MODEL), 1.0),
        'norm_w': gain(ks[1], D_MODEL),
        'w_in': nrm(ks[2], (D_MODEL, IN_DIM), D_MODEL ** -0.5),
        'conv_w': nrm(ks[3], (SSD_CONV_K, SSD_CONV_DIM), SSD_CONV_K ** -0.5),
        'conv_b': nrm(ks[4], (SSD_CONV_DIM,), 0.01),
        'dt_bias': dt_bias,
        'a_log': jnp.log(jax.random.uniform(ks[6], (SSD_HEADS,), f32, 1.0, 16.0)),
        'd_skip': gain(ks[7], SSD_HEADS),
        'ssd_norm_w': gain(ks[8], SSD_D_INNER),
        'q_norm_w': gain(ks[9], hd),
        'k_cmp_norm_w': gain(ks[10], hd),
        'k_slc_norm_w': gain(ks[11], hd),
        'k_win_norm_w': gain(ks[12], hd),
        'cmp_pe_k': nrm(ks[13], (CMP_BLOCK, hd), 0.02),
        'cmp_w1_k': nrm(ks[14], (CMP_BLOCK * hd, hd), (CMP_BLOCK * hd) ** -0.5),
        'cmp_b1_k': nrm(ks[15], (hd,), 0.01),
        'cmp_w2_k': nrm(ks[16], (hd, hd), hd ** -0.5),
        'cmp_pe_v': nrm(ks[17], (CMP_BLOCK, hd), 0.02),
        'cmp_w1_v': nrm(ks[18], (CMP_BLOCK * hd, hd), (CMP_BLOCK * hd) ** -0.5),
        'cmp_b1_v': nrm(ks[19], (hd,), 0.01),
        'cmp_w2_v': nrm(ks[20], (hd, hd), hd ** -0.5),
        'w_out_ssd': nrm(ks[21], (SSD_D_INNER, D_MODEL), SSD_D_INNER ** -0.5),
        'w_out_nsa': nrm(ks[22], (NSA_D, D_MODEL), NSA_D ** -0.5),
        'w_o': nrm(ks[23], (D_MODEL, D_MODEL), D_MODEL ** -0.5),
    }


def reference(x, norm_w, w_in, conv_w, conv_b, dt_bias, a_log, d_skip, ssd_norm_w,
              q_norm_w, k_cmp_norm_w, k_slc_norm_w, k_win_norm_w,
              cmp_pe_k, cmp_w1_k, cmp_b1_k, cmp_w2_k,
              cmp_pe_v, cmp_w1_v, cmp_b1_v, cmp_w2_v,
              w_out_ssd, w_out_nsa, w_o):
    sizes = [SSD_D_INNER, SSD_CONV_DIM, SSD_HEADS,
             NSA_D, NSA_KV_D, NSA_KV_D, NSA_KV_D, NSA_KV_D, NSA_KV_D, NSA_KV_D,
             NSA_D, NSA_HEADS * N_NSA_BRANCHES, D_MODEL, D_MODEL]
    offs = [int(o) for o in np.cumsum(sizes)[:-1]]
    for _ in range(DEPTH):
        h = rms_norm(x, norm_w)
        proj = h @ w_in
        (z_ssd, xbc, dt_raw, q, k_cmp, v_cmp, k_slc, v_slc, k_win, v_win,
         z_nsa, nsa_gate_logits, gl_ssd, gl_nsa) = jnp.split(proj, offs, axis=-1)
        y_ssd = ssd_branch(z_ssd, xbc, dt_raw, conv_w, conv_b, dt_bias, a_log, d_skip,
                           ssd_norm_w, w_out_ssd)
        y_nsa = nsa_branch(q, k_cmp, v_cmp, k_slc, v_slc, k_win, v_win, z_nsa, nsa_gate_logits,
                           q_norm_w, k_cmp_norm_w, k_slc_norm_w, k_win_norm_w,
                           cmp_pe_k, cmp_w1_k, cmp_b1_k, cmp_w2_k,
                           cmp_pe_v, cmp_w1_v, cmp_b1_v, cmp_w2_v, w_out_nsa)
        merged = jax.nn.sigmoid(gl_ssd) * y_ssd + jax.nn.sigmoid(gl_nsa) * y_nsa
        x = x + merged @ w_o
    return x
```

```python
import functools

import numpy as np
import jax
import jax.numpy as jnp
from jax import lax
from jax.experimental import pallas as pl
from jax.experimental.pallas import tpu as pltpu

D_MODEL = 2048
SSD_D_INNER = D_MODEL
SSD_HEAD_DIM = 64
SSD_HEADS = SSD_D_INNER // SSD_HEAD_DIM
SSD_GROUPS = 4
SSD_D_STATE = 128
SSD_CONV_K = 4
SSD_CHUNK = 128
SSD_BC_DIM = 2 * SSD_GROUPS * SSD_D_STATE
SSD_CONV_DIM = SSD_D_INNER + SSD_BC_DIM

NSA_HEADS = 16
NSA_HEAD_DIM = 128
NSA_KV_GROUPS = 2
NSA_HPG = NSA_HEADS // NSA_KV_GROUPS
NSA_D = NSA_HEADS * NSA_HEAD_DIM
NSA_KV_D = NSA_KV_GROUPS * NSA_HEAD_DIM
N_NSA_BRANCHES = 3
CMP_BLOCK = 32
CMP_STRIDE = 16
SEL_BLOCK = 64
SEL_TOPK = 8
WINDOW = 512
Q_BLOCK = 128

EPS = 1e-6
NEG_INF = -1e30
FORCED_SCORE = 1e9

LANES = 128
SUBLANES = 8
VMEM_LIMIT = 56 * 1024 * 1024

F32 = jnp.float32
BF16 = jnp.bfloat16
HIGHEST = lax.Precision.HIGHEST

C_ZSSD = 0
C_XS = 2048
C_Q = 4096
C_ZNSA = 6144
C_GLSSD = 8192
C_GLNSA = 10240
C_BC = 12288
C_KVCMP = 13312
C_KVSLC = 13824
C_KVWIN = 14336
N_MAIN = 14848
T_GATE = SSD_HEADS

_SIZES = [SSD_D_INNER, SSD_CONV_DIM, SSD_HEADS, NSA_D] + [NSA_KV_D] * 6 + [NSA_D, NSA_HEADS * N_NSA_BRANCHES, D_MODEL, D_MODEL]
_OFFS = [0] + [int(o) for o in np.cumsum(_SIZES)]


def _sigmoid(v):
    return 1.0 / (1.0 + jnp.exp(-v))


def _silu(v):
    return v * _sigmoid(v)


def _softplus(v):
    return jnp.maximum(v, 0.0) + jnp.log1p(jnp.exp(-jnp.abs(v)))


def _rms(v, w):
    ms = jnp.mean(v * v, axis=-1, keepdims=True)
    return v * lax.rsqrt(ms + EPS) * w


def _dot(a, b, precision=None):
    return jnp.dot(a, b, preferred_element_type=F32, precision=precision)


def _dot_nt(a, b, precision=None):
    return lax.dot_general(a, b, (((1,), (1,)), ((), ())), preferred_element_type=F32, precision=precision)


def _params(sem):
    return pltpu.CompilerParams(dimension_semantics=sem, vmem_limit_bytes=VMEM_LIMIT)


IN_TM = 1024
IN_TN = 512
IN_RC = 256


def _inproj_kernel(x_ref, nw_ref, w_ref, wt_ref, o_ref, t_ref, h_ref):
    j = pl.program_id(1)

    @pl.when(j == 0)
    def _():
        def body(r, carry):
            r0 = pl.multiple_of(r * IN_RC, IN_RC)
            h_ref[pl.ds(r0, IN_RC), :] = _rms(x_ref[pl.ds(r0, IN_RC), :], nw_ref[...]).astype(BF16)
            return carry

        lax.fori_loop(0, IN_TM // IN_RC, body, 0)
        t_ref[...] = _dot(h_ref[...], wt_ref[...])

    o_ref[...] = _dot(h_ref[...], w_ref[...]).astype(BF16)


def _inproj(x2, norm_w, w_main, w_tail):
    m = x2.shape[0]
    return pl.pallas_call(
        _inproj_kernel,
        grid=(m // IN_TM, N_MAIN // IN_TN),
        in_specs=[
            pl.BlockSpec((IN_TM, D_MODEL), lambda i, j: (i, 0)),
            pl.BlockSpec((1, D_MODEL), lambda i, j: (0, 0)),
            pl.BlockSpec((D_MODEL, IN_TN), lambda i, j: (0, j)),
            pl.BlockSpec((D_MODEL, LANES), lambda i, j: (0, 0)),
        ],
        out_specs=[
            pl.BlockSpec((IN_TM, IN_TN), lambda i, j: (i, j)),
            pl.BlockSpec((IN_TM, LANES), lambda i, j: (i, 0)),
        ],
        out_shape=[
            jax.ShapeDtypeStruct((m, N_MAIN), BF16),
            jax.ShapeDtypeStruct((m, LANES), F32),
        ],
        scratch_shapes=[pltpu.VMEM((IN_TM, D_MODEL), BF16)],
        compiler_params=_params(("arbitrary", "arbitrary")),
    )(x2, norm_w, w_main, w_tail)


CQ = SSD_CHUNK
PAIR = 2 * SSD_HEAD_DIM
N_PAIRS = SSD_HEADS // 2
PAIRS_PER_GROUP = N_PAIRS // SSD_GROUPS


def _ssd_kernel(z_ref, xs_ref, bc_ref, tail_ref, cwx_ref, cbx_ref, cwb_ref, cbb_ref, dtb_ref, alog_ref,
                dsk_ref, nw_ref, tri_ref, exp_ref, o_ref, xbuf, bbuf, xa_ref, ba_ref, st_ref, y_ref):
    c = pl.program_id(1)

    @pl.when(c == 0)
    def _():
        xbuf[0:SUBLANES, :] = jnp.zeros((SUBLANES, SSD_D_INNER), F32)
        bbuf[0:SUBLANES, :] = jnp.zeros((SUBLANES, SSD_BC_DIM), F32)
        st_ref[...] = jnp.zeros_like(st_ref)

    xbuf[SUBLANES:SUBLANES + CQ, :] = xs_ref[...].astype(F32)
    bbuf[SUBLANES:SUBLANES + CQ, :] = bc_ref[...].astype(F32)

    def conv(buf, w_ref, b_ref):
        acc = b_ref[...]
        for k in range(SSD_CONV_K):
            acc = acc + buf[pl.ds(SUBLANES - (SSD_CONV_K - 1) + k, CQ), :] * w_ref[k:k + 1, :]
        return _silu(acc)

    xa_ref[...] = conv(xbuf, cwx_ref, cbx_ref)
    ba_ref[...] = conv(bbuf, cwb_ref, cbb_ref)
    xbuf[0:SUBLANES, :] = xbuf[CQ:CQ + SUBLANES, :]
    bbuf[0:SUBLANES, :] = bbuf[CQ:CQ + SUBLANES, :]

    dt = _softplus(tail_ref[...] + dtb_ref[...])
    la = dt * (-jnp.exp(alog_ref[...]))
    lc = _dot(tri_ref[...], la, precision=HIGHEST)
    lc_t = lc.T
    dt_t = dt.T
    w_t = jnp.exp(lc_t[:, CQ - 1:CQ] - lc_t) * dt_t
    cd = jnp.exp(lc[CQ - 1:CQ, :])
    cdx = _dot(jnp.broadcast_to(cd, (SUBLANES, LANES)), exp_ref[...], precision=HIGHEST)

    row = lax.broadcasted_iota(jnp.int32, (CQ, CQ), 0)
    col = lax.broadcasted_iota(jnp.int32, (CQ, CQ), 1)
    causal = row >= col
    lo_half = col < SSD_HEAD_DIM

    for g in range(SSD_GROUPS):
        b_f = ba_ref[:, g * SSD_D_STATE:(g + 1) * SSD_D_STATE]
        c_f = ba_ref[:, (SSD_GROUPS + g) * SSD_D_STATE:(SSD_GROUPS + g + 1) * SSD_D_STATE]
        cb = _dot_nt(c_f.astype(BF16), b_f.astype(BF16))
        b_t = b_f.T
        for pp in range(PAIRS_PER_GROUP):
            p = g * PAIRS_PER_GROUP + pp
            lanes = slice(p * PAIR, (p + 1) * PAIR)
            xp = xa_ref[:, lanes].astype(BF16)
            s_prev = st_ref[:, lanes]
            s_prev_b = s_prev.astype(BF16)
            ys, sn = [], []
            for hh in range(2):
                h = 2 * p + hh
                lcol = jnp.broadcast_to(lc[:, h:h + 1], (CQ, CQ))
                lmat = jnp.exp(jnp.where(causal, lcol - lc_t[h:h + 1, :], -jnp.inf))
                m_h = (cb * lmat * dt_t[h:h + 1, :]).astype(BF16)
                c_s = (c_f * jnp.exp(lcol)).astype(BF16)
                ys.append(_dot(m_h, xp) + _dot(c_s, s_prev_b))
                w_h = (b_t * w_t[h:h + 1, :]).astype(BF16)
                sn.append(_dot(w_h, xp))
            y_ref[:, lanes] = jnp.where(lo_half, ys[0], ys[1])
            st_ref[:, lanes] = s_prev * cdx[0:1, lanes] + jnp.where(lo_half, sn[0], sn[1])

    y = y_ref[...] + xa_ref[...] * dsk_ref[...]
    y = y * _silu(z_ref[...].astype(F32))
    o_ref[...] = _rms(y, nw_ref[...]).astype(BF16)


def _ssd(proj, tail, cwx, cbx, cwb, cbb, dtb, alog, dsk, nw, tri, expand, batch, seq):
    nc = seq // CQ
    m = batch * seq

    def full(shape):
        return pl.BlockSpec(shape, lambda b, c: (0,) * len(shape))

    return pl.pallas_call(
        _ssd_kernel,
        grid=(batch, nc),
        in_specs=[
            pl.BlockSpec((CQ, SSD_D_INNER), lambda b, c: (b * nc + c, C_ZSSD // SSD_D_INNER)),
            pl.BlockSpec((CQ, SSD_D_INNER), lambda b, c: (b * nc + c, C_XS // SSD_D_INNER)),
            pl.BlockSpec((CQ, SSD_BC_DIM), lambda b, c: (b * nc + c, C_BC // SSD_BC_DIM)),
            pl.BlockSpec((CQ, LANES), lambda b, c: (b * nc + c, 0)),
            full((SSD_CONV_K, SSD_D_INNER)), full((1, SSD_D_INNER)),
            full((SSD_CONV_K, SSD_BC_DIM)), full((1, SSD_BC_DIM)),
            full((1, LANES)), full((1, LANES)),
            full((1, SSD_D_INNER)), full((1, SSD_D_INNER)),
            full((CQ, CQ)), full((LANES, SSD_D_INNER)),
        ],
        out_specs=pl.BlockSpec((CQ, SSD_D_INNER), lambda b, c: (b * nc + c, 0)),
        out_shape=jax.ShapeDtypeStruct((m, SSD_D_INNER), BF16),
        scratch_shapes=[
            pltpu.VMEM((CQ + SUBLANES, SSD_D_INNER), F32),
            pltpu.VMEM((CQ + SUBLANES, SSD_BC_DIM), F32),
            pltpu.VMEM((CQ, SSD_D_INNER), F32),
            pltpu.VMEM((CQ, SSD_BC_DIM), F32),
            pltpu.VMEM((SSD_D_STATE, SSD_D_INNER), F32),
            pltpu.VMEM((CQ, SSD_D_INNER), F32),
        ],
        compiler_params=_params(("arbitrary", "arbitrary")),
    )(proj, proj, proj, tail, cwx, cbx, cwb, cbb, dtb, alog, dsk, nw, tri, expand)


N_CMP_PAD = 128
CMP_HALF = CMP_BLOCK // CMP_STRIDE


def _cmp_kernel(kv_ref, pek_ref, w1k_ref, b1k_ref, w2k_ref, pev_ref, w1v_ref, b1v_ref, w2v_ref, knw_ref,
                kc_ref, vc_ref, buf):
    hd = NSA_HEAD_DIM
    for slab in range(2 * NSA_KV_GROUPS):
        buf[slab] = kv_ref[:, slab * hd:(slab + 1) * hd].astype(F32)
    for kind, (pe_ref, w1_ref, b1_ref, w2_ref, out_ref) in enumerate((
            (pek_ref, w1k_ref, b1k_ref, w2k_ref, kc_ref), (pev_ref, w1v_ref, b1v_ref, w2v_ref, vc_ref))):
        for g in range(NSA_KV_GROUPS):
            slab = kind * NSA_KV_GROUPS + g
            u = [jnp.zeros((N_CMP_PAD, hd), F32) for _ in range(CMP_HALF)]
            for r in range(CMP_STRIDE):
                xr = buf[slab, pl.ds(r, N_CMP_PAD, stride=CMP_STRIDE), :]
                for a in range(CMP_HALF):
                    l = a * CMP_STRIDE + r
                    u[a] = u[a] + _dot((xr + pe_ref[l:l + 1, :]).astype(BF16), w1_ref[l * hd:(l + 1) * hd, :])
            acc = u[0] + pltpu.roll(u[1], N_CMP_PAD - 1, 0) + b1_ref[...]
            out = _dot(_silu(acc).astype(BF16), w2_ref[...])
            if kind == 0:
                out = _rms(out, knw_ref[...])
            out_ref[0, :, g * hd:(g + 1) * hd] = out.astype(BF16)


def _compress(proj, pek, w1k, b1k, w2k, pev, w1v, b1v, w2v, knw, batch, seq):
    hd = NSA_HEAD_DIM

    def full(shape):
        return pl.BlockSpec(shape, lambda b: (0,) * len(shape))

    out_spec = pl.BlockSpec((1, N_CMP_PAD, NSA_KV_D), lambda b: (b, 0, 0))
    return pl.pallas_call(
        _cmp_kernel,
        grid=(batch,),
        in_specs=[
            pl.BlockSpec((seq, 2 * NSA_KV_D), lambda b: (b, C_KVCMP // (2 * NSA_KV_D))),
            full((CMP_BLOCK, hd)), full((CMP_BLOCK * hd, hd)), full((1, hd)), full((hd, hd)),
            full((CMP_BLOCK, hd)), full((CMP_BLOCK * hd, hd)), full((1, hd)), full((hd, hd)),
            full((1, hd)),
        ],
        out_specs=[out_spec, out_spec],
        out_shape=[jax.ShapeDtypeStruct((batch, N_CMP_PAD, NSA_KV_D), BF16)] * 2,
        scratch_shapes=[pltpu.VMEM((2 * NSA_KV_GROUPS, seq, hd), F32)],
        compiler_params=_params(("arbitrary",)),
    )(proj, pek, w1k, b1k, w2k, pev, w1v, b1v, w2v, knw)


QB = Q_BLOCK
GROWS = NSA_HPG * QB
SEL_TK = 256
WIN_TK = 128
N_SLC_PAD = 128
NORM_RC = 256


def _flash_step(s_all, bias, v_t, m_s, l_s, acc_s):
    for k in range(NSA_HPG):
        rows = slice(k * QB, (k + 1) * QB)
        s_k = s_all[rows, :] + bias
        m_prev = m_s[rows, :]
        m_new = jnp.maximum(m_prev, jnp.max(s_k, axis=-1, keepdims=True))
        alpha = jnp.exp(m_prev - m_new)
        p = jnp.exp(s_k - m_new)
        l_s[rows, :] = alpha * l_s[rows, :] + jnp.sum(p, axis=-1, keepdims=True)
        m_s[rows, :] = m_new
        acc_s[rows, :] = alpha * acc_s[rows, :] + _dot(p.astype(BF16), v_t)


def _nsa_kernel(q_ref, kvs_ref, kvw_ref, kc_ref, vc_ref, z_ref, tail_ref, qnw_ref, ksnw_ref, kwnw_ref, selmap_ref,
                o_ref, ks_s, kw_s, qg_s, oc_s, ms_s, ls_s, as_s, mw_s, lw_s, aw_s):
    i = pl.program_id(1)
    qs = i * QB
    hd = NSA_HEAD_DIM
    seq = kvs_ref.shape[0]
    scale = hd ** -0.5

    @pl.when(i == 0)
    def _():
        for src, dst, nw in ((kvs_ref, ks_s, ksnw_ref), (kvw_ref, kw_s, kwnw_ref)):
            for g in range(NSA_KV_GROUPS):
                def body(r, carry, src=src, dst=dst, nw=nw, g=g):
                    r0 = pl.multiple_of(r * NORM_RC, NORM_RC)
                    kk = src[pl.ds(r0, NORM_RC), g * hd:(g + 1) * hd].astype(F32)
                    dst[pl.ds(r0, NORM_RC), g * hd:(g + 1) * hd] = _rms(kk, nw[...]).astype(BF16)
                    return carry

                lax.fori_loop(0, seq // NORM_RC, body, 0)

    for h in range(NSA_HEADS):
        g, k = divmod(h, NSA_HPG)
        qh = q_ref[:, h * hd:(h + 1) * hd].astype(F32)
        qg_s[g, k * QB:(k + 1) * QB, :] = _rms(qh, qnw_ref[...]).astype(BF16)

    q_in = lax.broadcasted_iota(jnp.int32, (GROWS, LANES), 0) & (QB - 1)
    lane_g = lax.broadcasted_iota(jnp.int32, (GROWS, LANES), 1)
    mask_c = lane_g * CMP_STRIDE + (CMP_BLOCK - 1) <= qs + q_in

    jj = lax.broadcasted_iota(jnp.int32, (32, QB), 0)
    tq_t = qs + lax.broadcasted_iota(jnp.int32, (32, QB), 1)

    for g in range(NSA_KV_GROUPS):
        gl = slice(g * hd, (g + 1) * hd)
        qg = qg_s[g]

        s = _dot_nt(qg, kc_ref[0, :, gl]) * scale
        s = jnp.where(mask_c, s, NEG_INF)
        e = jnp.exp(s - jnp.max(s, axis=-1, keepdims=True))
        p_cmp = jnp.where(mask_c, e / jnp.sum(e, axis=-1, keepdims=True), 0.0)
        oc_s[...] = _dot(p_cmp.astype(BF16), vc_ref[0, :, gl])

        p_sum = p_cmp[0:QB, :]
        for k in range(1, NSA_HPG):
            p_sum = p_sum + p_cmp[k * QB:(k + 1) * QB, :]
        imp = _dot_nt(selmap_ref[...], p_sum, precision=HIGHEST)[0:32, :]
        forced = (jj == (tq_t >> 6)) | (jj == 0)
        imp = jnp.where(forced, FORCED_SCORE, jnp.where(jj * SEL_BLOCK <= tq_t, imp, -1.0))
        rank = jnp.zeros((32, QB), jnp.int32)
        for j2 in range(32):
            other = imp[j2:j2 + 1, :]
            beats = (other > imp) | ((other == imp) & (jj > j2))
            rank = rank + jnp.where(beats, 1, 0)
        sel_t = jnp.where(rank < SEL_TOPK, 1.0, 0.0).astype(F32)
        sel_t = jnp.concatenate([sel_t, jnp.zeros((N_SLC_PAD - 32, QB), F32)], axis=0)
        sel = sel_t.T.astype(BF16)

        ms_s[...] = jnp.full_like(ms_s, NEG_INF)
        ls_s[...] = jnp.zeros_like(ls_s)
        as_s[...] = jnp.zeros_like(as_s)

        def sel_body(t, carry, g=g, gl=gl, qg=qg, sel=sel):
            k0 = pl.multiple_of(t * SEL_TK, SEL_TK)
            k_t = ks_s[pl.ds(k0, SEL_TK), gl]
            v_t = kvs_ref[pl.ds(k0, SEL_TK), NSA_KV_D + g * hd:NSA_KV_D + (g + 1) * hd]
            s_all = _dot_nt(qg, k_t) * scale
            jrow = lax.broadcasted_iota(jnp.int32, (N_SLC_PAD, SEL_TK), 0)
            key = lax.broadcasted_iota(jnp.int32, (N_SLC_PAD, SEL_TK), 1)
            e_t = jnp.where(jrow == t * (SEL_TK // SEL_BLOCK) + (key >> 6), 1.0, 0.0).astype(BF16)
            picked = _dot(sel, e_t)
            qrow = lax.broadcasted_iota(jnp.int32, (QB, SEL_TK), 0)
            kcol = lax.broadcasted_iota(jnp.int32, (QB, SEL_TK), 1)
            ok = (picked > 0.5) & (k0 + kcol <= qs + qrow)
            _flash_step(s_all, jnp.where(ok, 0.0, NEG_INF), v_t, ms_s, ls_s, as_s)
            return carry

        lax.fori_loop(0, (qs + QB + SEL_TK - 1) // SEL_TK, sel_body, 0)

        mw_s[...] = jnp.full_like(mw_s, NEG_INF)
        lw_s[...] = jnp.zeros_like(lw_s)
        aw_s[...] = jnp.zeros_like(aw_s)

        def win_body(t, carry, g=g, gl=gl, qg=qg):
            k0 = pl.multiple_of((i - t) * WIN_TK, WIN_TK)
            k_t = kw_s[pl.ds(k0, WIN_TK), gl]
            v_t = kvw_ref[pl.ds(k0, WIN_TK), NSA_KV_D + g * hd:NSA_KV_D + (g + 1) * hd]
            s_all = _dot_nt(qg, k_t) * scale
            qrow = lax.broadcasted_iota(jnp.int32, (QB, WIN_TK), 0)
            kcol = lax.broadcasted_iota(jnp.int32, (QB, WIN_TK), 1)
            rel = (k0 + kcol) - (qs + qrow)
            ok = (rel <= 0) & (rel > -WINDOW)
            _flash_step(s_all, jnp.where(ok, 0.0, NEG_INF), v_t, mw_s, lw_s, aw_s)
            return carry

        lax.fori_loop(0, jnp.minimum(i, WINDOW // WIN_TK) + 1, win_body, 0)

        for k in range(NSA_HPG):
            h = g * NSA_HPG + k
            rows = slice(k * QB, (k + 1) * QB)
            c0 = T_GATE + h * N_NSA_BRANCHES
            gates = _sigmoid(tail_ref[:, c0:c0 + N_NSA_BRANCHES])
            o = (gates[:, 0:1] * oc_s[rows, :]
                 + gates[:, 1:2] * (as_s[rows, :] / ls_s[rows, :])
                 + gates[:, 2:3] * (aw_s[rows, :] / lw_s[rows, :]))
            zz = z_ref[:, h * hd:(h + 1) * hd].astype(F32)
            o_ref[:, h * hd:(h + 1) * hd] = (o * _silu(zz)).astype(BF16)


def _nsa(proj, tail, kc, vc, qnw, ksnw, kwnw, selmap_t, batch, seq):
    nq = seq // QB
    hd = NSA_HEAD_DIM
    m = batch * seq

    def full(shape):
        return pl.BlockSpec(shape, lambda b, i: (0,) * len(shape))

    return pl.pallas_call(
        _nsa_kernel,
        grid=(batch, nq),
        in_specs=[
            pl.BlockSpec((QB, NSA_D), lambda b, i: (b * nq + i, C_Q // NSA_D)),
            pl.BlockSpec((seq, 2 * NSA_KV_D), lambda b, i: (b, C_KVSLC // (2 * NSA_KV_D))),
            pl.BlockSpec((seq, 2 * NSA_KV_D), lambda b, i: (b, C_KVWIN // (2 * NSA_KV_D))),
            pl.BlockSpec((1, N_CMP_PAD, NSA_KV_D), lambda b, i: (b, 0, 0)),
            pl.BlockSpec((1, N_CMP_PAD, NSA_KV_D), lambda b, i: (b, 0, 0)),
            pl.BlockSpec((QB, NSA_D), lambda b, i: (b * nq + i, C_ZNSA // NSA_D)),
            pl.BlockSpec((QB, LANES), lambda b, i: (b * nq + i, 0)),
            full((1, hd)), full((1, hd)), full((1, hd)),
            full((N_SLC_PAD, N_CMP_PAD)),
        ],
        out_specs=pl.BlockSpec((QB, NSA_D), lambda b, i: (b * nq + i, 0)),
        out_shape=jax.ShapeDtypeStruct((m, NSA_D), BF16),
        scratch_shapes=[
            pltpu.VMEM((seq, NSA_KV_D), BF16),
            pltpu.VMEM((seq, NSA_KV_D), BF16),
            pltpu.VMEM((NSA_KV_GROUPS, GROWS, hd), BF16),
            pltpu.VMEM((GROWS, hd), F32),
            pltpu.VMEM((GROWS, 1), F32), pltpu.VMEM((GROWS, 1), F32), pltpu.VMEM((GROWS, hd), F32),
            pltpu.VMEM((GROWS, 1), F32), pltpu.VMEM((GROWS, 1), F32), pltpu.VMEM((GROWS, hd), F32),
        ],
        compiler_params=_params(("arbitrary", "arbitrary")),
    )(proj, proj, proj, kc, vc, proj, tail, qnw, ksnw, kwnw, selmap_t)


OUT_TM = 1024
OUT_TN = 512


def _merge_kernel(ys_ref, yn_ref, gs_ref, gn_ref, ws_ref, wn_ref, o_ref):
    a = _dot(ys_ref[...], ws_ref[...])
    b = _dot(yn_ref[...], wn_ref[...])
    merged = _sigmoid(gs_ref[...].astype(F32)) * a + _sigmoid(gn_ref[...].astype(F32)) * b
    o_ref[...] = merged.astype(BF16)


def _merge(y_ssd, y_nsa, proj, w_s, w_n):
    m = y_ssd.shape[0]
    nb = D_MODEL // OUT_TN
    return pl.pallas_call(
        _merge_kernel,
        grid=(nb, m // OUT_TM),
        in_specs=[
            pl.BlockSpec((OUT_TM, D_MODEL), lambda j, i: (i, 0)),
            pl.BlockSpec((OUT_TM, D_MODEL), lambda j, i: (i, 0)),
            pl.BlockSpec((OUT_TM, OUT_TN), lambda j, i: (i, C_GLSSD // OUT_TN + j)),
            pl.BlockSpec((OUT_TM, OUT_TN), lambda j, i: (i, C_GLNSA // OUT_TN + j)),
            pl.BlockSpec((D_MODEL, OUT_TN), lambda j, i: (0, j)),
            pl.BlockSpec((D_MODEL, OUT_TN), lambda j, i: (0, j)),
        ],
        out_specs=pl.BlockSpec((OUT_TM, OUT_TN), lambda j, i: (i, j)),
        out_shape=jax.ShapeDtypeStruct((m, D_MODEL), BF16),
        compiler_params=_params(("arbitrary", "arbitrary")),
    )(y_ssd, y_nsa, proj, proj, w_s, w_n)


def _final_kernel(m_ref, x_ref, w_ref, o_ref):
    o_ref[...] = x_ref[...] + _dot(m_ref[...], w_ref[...])


def _final(merged, x2, w_o):
    m = merged.shape[0]
    nb = D_MODEL // OUT_TN
    return pl.pallas_call(
        _final_kernel,
        grid=(nb, m // OUT_TM),
        in_specs=[
            pl.BlockSpec((OUT_TM, D_MODEL), lambda j, i: (i, 0)),
            pl.BlockSpec((OUT_TM, OUT_TN), lambda j, i: (i, j)),
            pl.BlockSpec((D_MODEL, OUT_TN), lambda j, i: (0, j)),
        ],
        out_specs=pl.BlockSpec((OUT_TM, OUT_TN), lambda j, i: (i, j)),
        out_shape=jax.ShapeDtypeStruct((m, D_MODEL), F32),
        compiler_params=_params(("arbitrary", "arbitrary")),
    )(merged, x2, w_o)


def _w_in_sections(w_in):
    sec = {name: w_in[:, _OFFS[i]:_OFFS[i + 1]] for i, name in enumerate(
        ("z_ssd", "xbc", "dt", "q", "k_cmp", "v_cmp", "k_slc", "v_slc", "k_win", "v_win",
         "z_nsa", "gates", "gl_ssd", "gl_nsa"))}
    main = jnp.concatenate([
        sec["z_ssd"], sec["xbc"][:, :SSD_D_INNER], sec["q"], sec["z_nsa"], sec["gl_ssd"], sec["gl_nsa"],
        sec["xbc"][:, SSD_D_INNER:], sec["k_cmp"], sec["v_cmp"], sec["k_slc"], sec["v_slc"],
        sec["k_win"], sec["v_win"]], axis=1).astype(BF16)
    pad = LANES - SSD_HEADS - NSA_HEADS * N_NSA_BRANCHES
    tail = jnp.concatenate([sec["dt"], sec["gates"], jnp.zeros((D_MODEL, pad), w_in.dtype)], axis=1).astype(BF16)
    return main, tail


def _constants():
    tri = np.tril(np.ones((CQ, CQ), np.float32))
    expand = np.zeros((LANES, SSD_D_INNER), np.float32)
    for h in range(SSD_HEADS):
        expand[h, h * SSD_HEAD_DIM:(h + 1) * SSD_HEAD_DIM] = 1.0
    n_cmp = N_CMP_PAD - 1
    ci = np.arange(n_cmp)[:, None] * CMP_STRIDE
    sj = np.arange(32)[None, :] * SEL_BLOCK
    sel_map = ((ci < sj + SEL_BLOCK) & (ci + CMP_BLOCK > sj)).astype(np.float32)
    selmap_t = np.zeros((N_SLC_PAD, N_CMP_PAD), np.float32)
    selmap_t[:32, :n_cmp] = sel_map.T
    return jnp.asarray(tri), jnp.asarray(expand), jnp.asarray(selmap_t)


def _pad_lanes(v):
    return jnp.concatenate([v.astype(F32), jnp.zeros((LANES - v.shape[0],), F32)])[None, :]


def kernel(x, norm_w, w_in, conv_w, conv_b, dt_bias, a_log, d_skip, ssd_norm_w, q_norm_w, k_cmp_norm_w,
           k_slc_norm_w, k_win_norm_w, cmp_pe_k, cmp_w1_k, cmp_b1_k, cmp_w2_k, cmp_pe_v, cmp_w1_v, cmp_b1_v,
           cmp_w2_v, w_out_ssd, w_out_nsa, w_o):
    batch, seq, _ = x.shape
    assert seq == 2048 and (batch * seq) % IN_TM == 0
    x2 = x.reshape(batch * seq, D_MODEL)
    w_main, w_tail = _w_in_sections(w_in)
    tri, expand, selmap_t = _constants()

    proj, tail = _inproj(x2, norm_w[None, :], w_main, w_tail)

    y_ssd = _ssd(proj, tail,
                 conv_w[:, :SSD_D_INNER], conv_b[None, :SSD_D_INNER],
                 conv_w[:, SSD_D_INNER:], conv_b[None, SSD_D_INNER:],
                 _pad_lanes(dt_bias), _pad_lanes(a_log),
                 jnp.repeat(d_skip, SSD_HEAD_DIM)[None, :], ssd_norm_w[None, :],
                 tri, expand, batch, seq)

    kc, vc = _compress(proj, cmp_pe_k, cmp_w1_k.astype(BF16), cmp_b1_k[None, :], cmp_w2_k.astype(BF16),
                       cmp_pe_v, cmp_w1_v.astype(BF16), cmp_b1_v[None, :], cmp_w2_v.astype(BF16),
                       k_cmp_norm_w[None, :], batch, seq)

    y_nsa = _nsa(proj, tail, kc, vc, q_norm_w[None, :], k_slc_norm_w[None, :], k_win_norm_w[None, :],
                 selmap_t, batch, seq)

    merged = _merge(y_ssd, y_nsa, proj, w_out_ssd.astype(BF16), w_out_nsa.astype(BF16))
    out = _final(merged, x2, w_o.astype(BF16))
    return out.reshape(batch, seq, D_MODEL)
```

```python
import functools

import numpy as np
import jax
import jax.numpy as jnp
from jax import lax
from jax.experimental import pallas as pl
from jax.experimental.pallas import tpu as pltpu

D_MODEL = 2048
SSD_D_INNER = D_MODEL
SSD_HEAD_DIM = 64
SSD_HEADS = SSD_D_INNER // SSD_HEAD_DIM
SSD_GROUPS = 4
SSD_D_STATE = 128
SSD_CONV_K = 4
SSD_CHUNK = 128
SSD_BC_DIM = 2 * SSD_GROUPS * SSD_D_STATE
SSD_CONV_DIM = SSD_D_INNER + SSD_BC_DIM

NSA_HEADS = 16
NSA_HEAD_DIM = 128
NSA_KV_GROUPS = 2
NSA_HPG = NSA_HEADS // NSA_KV_GROUPS
NSA_D = NSA_HEADS * NSA_HEAD_DIM
NSA_KV_D = NSA_KV_GROUPS * NSA_HEAD_DIM
N_NSA_BRANCHES = 3
CMP_BLOCK = 32
CMP_STRIDE = 16
SEL_BLOCK = 64
SEL_TOPK = 8
WINDOW = 512
Q_BLOCK = 128

EPS = 1e-6
NEG_INF = -1e30
FORCED_SCORE = 1e9

LANES = 128
SUBLANES = 8
VMEM_LIMIT = 56 * 1024 * 1024

F32 = jnp.float32
BF16 = jnp.bfloat16
HIGHEST = lax.Precision.HIGHEST

C_ZSSD = 0
C_XS = 2048
C_Q = 4096
C_ZNSA = 6144
C_GLSSD = 8192
C_GLNSA = 10240
C_BC = 12288
C_KVCMP = 13312
C_KVSLC = 13824
C_KVWIN = 14336
N_MAIN = 14848
T_GATE = SSD_HEADS

_SIZES = [SSD_D_INNER, SSD_CONV_DIM, SSD_HEADS, NSA_D] + [NSA_KV_D] * 6 + [NSA_D, NSA_HEADS * N_NSA_BRANCHES, D_MODEL, D_MODEL]
_OFFS = [0] + [int(o) for o in np.cumsum(_SIZES)]


def _sigmoid(v):
    return 1.0 / (1.0 + jnp.exp(-v))


def _silu(v):
    return v * _sigmoid(v)


def _softplus(v):
    return jnp.maximum(v, 0.0) + jnp.log1p(jnp.exp(-jnp.abs(v)))


def _rms(v, w):
    ms = jnp.mean(v * v, axis=-1, keepdims=True)
    return v * lax.rsqrt(ms + EPS) * w


def _dot(a, b, precision=None):
    return jnp.dot(a, b, preferred_element_type=F32, precision=precision)


def _dot_nt(a, b, precision=None):
    return lax.dot_general(a, b, (((1,), (1,)), ((), ())), preferred_element_type=F32, precision=precision)


def _params(sem):
    return pltpu.CompilerParams(dimension_semantics=sem, vmem_limit_bytes=VMEM_LIMIT)


IN_TM = 1024
IN_TN = 512
IN_RC = 256


def _inproj_kernel(x_ref, nw_ref, w_ref, wt_ref, o_ref, t_ref, h_ref):
    j = pl.program_id(1)

    @pl.when(j == 0)
    def _():
        def body(r, carry):
            r0 = pl.multiple_of(r * IN_RC, IN_RC)
            h_ref[pl.ds(r0, IN_RC), :] = _rms(x_ref[pl.ds(r0, IN_RC), :], nw_ref[...]).astype(BF16)
            return carry

        lax.fori_loop(0, IN_TM // IN_RC, body, 0)
        t_ref[...] = _dot(h_ref[...], wt_ref[...])

    o_ref[...] = _dot(h_ref[...], w_ref[...]).astype(BF16)


def _inproj(x2, norm_w, w_main, w_tail):
    m = x2.shape[0]
    return pl.pallas_call(
        _inproj_kernel,
        grid=(m // IN_TM, N_MAIN // IN_TN),
        in_specs=[
            pl.BlockSpec((IN_TM, D_MODEL), lambda i, j: (i, 0)),
            pl.BlockSpec((1, D_MODEL), lambda i, j: (0, 0)),
            pl.BlockSpec((D_MODEL, IN_TN), lambda i, j: (0, j)),
            pl.BlockSpec((D_MODEL, LANES), lambda i, j: (0, 0)),
        ],
        out_specs=[
            pl.BlockSpec((IN_TM, IN_TN), lambda i, j: (i, j)),
            pl.BlockSpec((IN_TM, LANES), lambda i, j: (i, 0)),
        ],
        out_shape=[
            jax.ShapeDtypeStruct((m, N_MAIN), BF16),
            jax.ShapeDtypeStruct((m, LANES), F32),
        ],
        scratch_shapes=[pltpu.VMEM((IN_TM, D_MODEL), BF16)],
        compiler_params=_params(("arbitrary", "arbitrary")),
    )(x2, norm_w, w_main, w_tail)


CQ = SSD_CHUNK
PAIR = 2 * SSD_HEAD_DIM
N_PAIRS = SSD_HEADS // 2
PAIRS_PER_GROUP = N_PAIRS // SSD_GROUPS


def _ssd_kernel(z_ref, xs_ref, bc_ref, tail_ref, cwx_ref, cbx_ref, cwb_ref, cbb_ref, dtb_ref, alog_ref,
                dsk_ref, nw_ref, tri_ref, exp_ref, o_ref, xbuf, bbuf, xa_ref, ba_ref, st_ref, y_ref):
    c = pl.program_id(1)

    @pl.when(c == 0)
    def _():
        xbuf[0:SUBLANES, :] = jnp.zeros((SUBLANES, SSD_D_INNER), F32)
        bbuf[0:SUBLANES, :] = jnp.zeros((SUBLANES, SSD_BC_DIM), F32)
        st_ref[...] = jnp.zeros_like(st_ref)

    xbuf[SUBLANES:SUBLANES + CQ, :] = xs_ref[...].astype(F32)
    bbuf[SUBLANES:SUBLANES + CQ, :] = bc_ref[...].astype(F32)

    def conv(buf, w_ref, b_ref):
        acc = b_ref[...]
        for k in range(SSD_CONV_K):
            acc = acc + buf[pl.ds(SUBLANES - (SSD_CONV_K - 1) + k, CQ), :] * w_ref[k:k + 1, :]
        return _silu(acc)

    xa_ref[...] = conv(xbuf, cwx_ref, cbx_ref)
    ba_ref[...] = conv(bbuf, cwb_ref, cbb_ref)
    xbuf[0:SUBLANES, :] = xbuf[CQ:CQ + SUBLANES, :]
    bbuf[0:SUBLANES, :] = bbuf[CQ:CQ + SUBLANES, :]

    dt = _softplus(tail_ref[...] + dtb_ref[...])
    la = dt * (-jnp.exp(alog_ref[...]))
    lc = _dot(tri_ref[...], la, precision=HIGHEST)
    lc_t = lc.T
    dt_t = dt.T
    w_t = jnp.exp(lc_t[:, CQ - 1:CQ] - lc_t) * dt_t
    cd = jnp.exp(lc[CQ - 1:CQ, :])
    cdx = _dot(jnp.broadcast_to(cd, (SUBLANES, LANES)), exp_ref[...], precision=HIGHEST)

    row = lax.broadcasted_iota(jnp.int32, (CQ, CQ), 0)
    col = lax.broadcasted_iota(jnp.int32, (CQ, CQ), 1)
    causal = row >= col
    lo_half = col < SSD_HEAD_DIM

    for g in range(SSD_GROUPS):
        b_f = ba_ref[:, g * SSD_D_STATE:(g + 1) * SSD_D_STATE]
        c_f = ba_ref[:, (SSD_GROUPS + g) * SSD_D_STATE:(SSD_GROUPS + g + 1) * SSD_D_STATE]
        cb = _dot_nt(c_f.astype(BF16), b_f.astype(BF16))
        b_t = b_f.T
        for pp in range(PAIRS_PER_GROUP):
            p = g * PAIRS_PER_GROUP + pp
            lanes = slice(p * PAIR, (p + 1) * PAIR)
            xp = xa_ref[:, lanes].astype(BF16)
            s_prev = st_ref[:, lanes]
            s_prev_b = s_prev.astype(BF16)
            ys, sn = [], []
            for hh in range(2):
                h = 2 * p + hh
                lcol = jnp.broadcast_to(lc[:, h:h + 1], (CQ, CQ))
                lmat = jnp.exp(jnp.where(causal, lcol - lc_t[h:h + 1, :], -jnp.inf))
                m_h = (cb * lmat * dt_t[h:h + 1, :]).astype(BF16)
                c_s = (c_f * jnp.exp(lcol)).astype(BF16)
                ys.append(_dot(m_h, xp) + _dot(c_s, s_prev_b))
                w_h = (b_t * w_t[h:h + 1, :]).astype(BF16)
                sn.append(_dot(w_h, xp))
            y_ref[:, lanes] = jnp.where(lo_half, ys[0], ys[1])
            st_ref[:, lanes] = s_prev * cdx[0:1, lanes] + jnp.where(lo_half, sn[0], sn[1])

    y = y_ref[...] + xa_ref[...] * dsk_ref[...]
    y = y * _silu(z_ref[...].astype(F32))
    o_ref[...] = _rms(y, nw_ref[...]).astype(BF16)


def _ssd(proj, tail, cwx, cbx, cwb, cbb, dtb, alog, dsk, nw, tri, expand, batch, seq):
    nc = seq // CQ
    m = batch * seq

    def full(shape):
        return pl.BlockSpec(shape, lambda b, c: (0,) * len(shape))

    return pl.pallas_call(
        _ssd_kernel,
        grid=(batch, nc),
        in_specs=[
            pl.BlockSpec((CQ, SSD_D_INNER), lambda b, c: (b * nc + c, C_ZSSD // SSD_D_INNER)),
            pl.BlockSpec((CQ, SSD_D_INNER), lambda b, c: (b * nc + c, C_XS // SSD_D_INNER)),
            pl.BlockSpec((CQ, SSD_BC_DIM), lambda b, c: (b * nc + c, C_BC // SSD_BC_DIM)),
            pl.BlockSpec((CQ, LANES), lambda b, c: (b * nc + c, 0)),
            full((SSD_CONV_K, SSD_D_INNER)), full((1, SSD_D_INNER)),
            full((SSD_CONV_K, SSD_BC_DIM)), full((1, SSD_BC_DIM)),
            full((1, LANES)), full((1, LANES)),
            full((1, SSD_D_INNER)), full((1, SSD_D_INNER)),
            full((CQ, CQ)), full((LANES, SSD_D_INNER)),
        ],
        out_specs=pl.BlockSpec((CQ, SSD_D_INNER), lambda b, c: (b * nc + c, 0)),
        out_shape=jax.ShapeDtypeStruct((m, SSD_D_INNER), BF16),
        scratch_shapes=[
            pltpu.VMEM((CQ + SUBLANES, SSD_D_INNER), F32),
            pltpu.VMEM((CQ + SUBLANES, SSD_BC_DIM), F32),
            pltpu.VMEM((CQ, SSD_D_INNER), F32),
            pltpu.VMEM((CQ, SSD_BC_DIM), F32),
            pltpu.VMEM((SSD_D_STATE, SSD_D_INNER), F32),
            pltpu.VMEM((CQ, SSD_D_INNER), F32),
        ],
        compiler_params=_params(("arbitrary", "arbitrary")),
    )(proj, proj, proj, tail, cwx, cbx, cwb, cbb, dtb, alog, dsk, nw, tri, expand)


N_CMP_PAD = 128
CMP_HALF = CMP_BLOCK // CMP_STRIDE


def _cmp_kernel(kv_ref, pek_ref, w1k_ref, b1k_ref, w2k_ref, pev_ref, w1v_ref, b1v_ref, w2v_ref, knw_ref,
                kc_ref, vc_ref, buf):
    hd = NSA_HEAD_DIM
    for slab in range(2 * NSA_KV_GROUPS):
        buf[slab] = kv_ref[:, slab * hd:(slab + 1) * hd].astype(F32)
    for kind, (pe_ref, w1_ref, b1_ref, w2_ref, out_ref) in enumerate((
            (pek_ref, w1k_ref, b1k_ref, w2k_ref, kc_ref), (pev_ref, w1v_ref, b1v_ref, w2v_ref, vc_ref))):
        for g in range(NSA_KV_GROUPS):
            slab = kind * NSA_KV_GROUPS + g
            u = [jnp.zeros((N_CMP_PAD, hd), F32) for _ in range(CMP_HALF)]
            for r in range(CMP_STRIDE):
                xr = buf[slab, pl.ds(r, N_CMP_PAD, stride=CMP_STRIDE), :]
                for a in range(CMP_HALF):
                    l = a * CMP_STRIDE + r
                    u[a] = u[a] + _dot((xr + pe_ref[l:l + 1, :]).astype(BF16), w1_ref[l * hd:(l + 1) * hd, :])
            acc = u[0] + pltpu.roll(u[1], N_CMP_PAD - 1, 0) + b1_ref[...]
            out = _dot(_silu(acc).astype(BF16), w2_ref[...])
            if kind == 0:
                out_ref[0, :, g * hd:(g + 1) * hd] = _rms(out, knw_ref[...]).astype(BF16)
            else:
                out_ref[0, g * hd:(g + 1) * hd, :] = out.T.astype(BF16)


def _compress(proj, pek, w1k, b1k, w2k, pev, w1v, b1v, w2v, knw, batch, seq):
    hd = NSA_HEAD_DIM

    def full(shape):
        return pl.BlockSpec(shape, lambda b: (0,) * len(shape))

    return pl.pallas_call(
        _cmp_kernel,
        grid=(batch,),
        in_specs=[
            pl.BlockSpec((seq, 2 * NSA_KV_D), lambda b: (b, C_KVCMP // (2 * NSA_KV_D))),
            full((CMP_BLOCK, hd)), full((CMP_BLOCK * hd, hd)), full((1, hd)), full((hd, hd)),
            full((CMP_BLOCK, hd)), full((CMP_BLOCK * hd, hd)), full((1, hd)), full((hd, hd)),
            full((1, hd)),
        ],
        out_specs=[pl.BlockSpec((1, N_CMP_PAD, NSA_KV_D), lambda b: (b, 0, 0)),
                   pl.BlockSpec((1, NSA_KV_D, N_CMP_PAD), lambda b: (b, 0, 0))],
        out_shape=[jax.ShapeDtypeStruct((batch, N_CMP_PAD, NSA_KV_D), BF16),
                   jax.ShapeDtypeStruct((batch, NSA_KV_D, N_CMP_PAD), BF16)],
        scratch_shapes=[pltpu.VMEM((2 * NSA_KV_GROUPS, seq, hd), F32)],
        compiler_params=_params(("arbitrary",)),
    )(proj, pek, w1k, b1k, w2k, pev, w1v, b1v, w2v, knw)


QB = Q_BLOCK
GROWS = NSA_HPG * QB
SEL_TK = 256
WIN_TK = 128
N_SLC = 32
N_SLC_PAD = 128
NORM_RC = 256
KT = 128


def _flash_step(s_all, bias, v_tt, m_s, l_s, acc_s):
    ps, alphas = [], []
    for k in range(NSA_HPG):
        cols = slice(k * QB, (k + 1) * QB)
        s_k = s_all[:, cols] + bias
        m_prev = m_s[:, cols]
        m_new = jnp.maximum(m_prev, jnp.max(s_k, axis=0, keepdims=True))
        alpha = jnp.exp(m_prev - m_new)
        p = jnp.exp(s_k - m_new)
        l_s[:, cols] = alpha * l_s[:, cols] + jnp.sum(p, axis=0, keepdims=True)
        m_s[:, cols] = m_new
        ps.append(p.astype(BF16))
        alphas.append(alpha)
    pv = _dot(v_tt, jnp.concatenate(ps, axis=1))
    for k in range(NSA_HPG):
        cols = slice(k * QB, (k + 1) * QB)
        acc_s[:, cols] = alphas[k] * acc_s[:, cols] + pv[:, cols]


def _nsa_kernel(q_ref, kvs_ref, kvw_ref, kc_ref, vct_ref, z_ref, tail_ref, qnw_ref, ksnw_ref, kwnw_ref, selmap_ref,
                o_ref, ks_s, kw_s, vst_s, vwt_s, qg_s, sel_s, oc_s, ms_s, ls_s, as_s, mw_s, lw_s, aw_s):
    i = pl.program_id(1)
    qs = i * QB
    hd = NSA_HEAD_DIM
    seq = kvs_ref.shape[0]
    scale = hd ** -0.5

    @pl.when(i == 0)
    def _():
        for src, dst, dst_t, nw in ((kvs_ref, ks_s, vst_s, ksnw_ref), (kvw_ref, kw_s, vwt_s, kwnw_ref)):
            for g in range(NSA_KV_GROUPS):
                def body(r, carry, src=src, dst=dst, nw=nw, g=g):
                    r0 = pl.multiple_of(r * NORM_RC, NORM_RC)
                    kk = src[pl.ds(r0, NORM_RC), g * hd:(g + 1) * hd].astype(F32)
                    dst[pl.ds(r0, NORM_RC), g * hd:(g + 1) * hd] = _rms(kk, nw[...]).astype(BF16)
                    return carry

                lax.fori_loop(0, seq // NORM_RC, body, 0)

                def body_t(r, carry, src=src, dst_t=dst_t, g=g):
                    r0 = pl.multiple_of(r * KT, KT)
                    vv = src[pl.ds(r0, KT), NSA_KV_D + g * hd:NSA_KV_D + (g + 1) * hd].astype(F32)
                    dst_t[r, g * hd:(g + 1) * hd, :] = vv.T.astype(BF16)
                    return carry

                lax.fori_loop(0, seq // KT, body_t, 0)

    for h in range(NSA_HEADS):
        g, k = divmod(h, NSA_HPG)
        qh = q_ref[:, h * hd:(h + 1) * hd].astype(F32)
        qg_s[g, k * QB:(k + 1) * QB, :] = _rms(qh, qnw_ref[...]).astype(BF16)

    gates_t = _sigmoid(tail_ref[...]).T

    c_idx = lax.broadcasted_iota(jnp.int32, (N_CMP_PAD, GROWS), 0)
    q_in = lax.broadcasted_iota(jnp.int32, (N_CMP_PAD, GROWS), 1) & (QB - 1)
    mask_c = c_idx * CMP_STRIDE + (CMP_BLOCK - 1) <= qs + q_in

    jj = lax.broadcasted_iota(jnp.int32, (N_SLC, QB), 0)
    tq_t = qs + lax.broadcasted_iota(jnp.int32, (N_SLC, QB), 1)

    for g in range(NSA_KV_GROUPS):
        gl = slice(g * hd, (g + 1) * hd)
        qg = qg_s[g]

        s = _dot_nt(kc_ref[0, :, gl], qg) * scale
        s = jnp.where(mask_c, s, NEG_INF)
        e = jnp.exp(s - jnp.max(s, axis=0, keepdims=True))
        p_cmp = jnp.where(mask_c, e / jnp.sum(e, axis=0, keepdims=True), 0.0)
        oc_s[...] = _dot(vct_ref[0, gl, :], p_cmp.astype(BF16))

        p_sum = p_cmp[:, 0:QB]
        for k in range(1, NSA_HPG):
            p_sum = p_sum + p_cmp[:, k * QB:(k + 1) * QB]
        imp = _dot(selmap_ref[...], p_sum, precision=HIGHEST)[0:N_SLC, :]
        forced = (jj == (tq_t >> 6)) | (jj == 0)
        imp = jnp.where(forced, FORCED_SCORE, jnp.where(jj * SEL_BLOCK <= tq_t, imp, -1.0))
        rank = jnp.zeros((N_SLC, QB), jnp.int32)
        for j2 in range(N_SLC):
            other = imp[j2:j2 + 1, :]
            beats = (other > imp) | ((other == imp) & (jj > j2))
            rank = rank + jnp.where(beats, 1, 0)
        sel_s[...] = jnp.where(rank < SEL_TOPK, 0.0, NEG_INF).astype(F32)

        ms_s[...] = jnp.full_like(ms_s, NEG_INF)
        ls_s[...] = jnp.zeros_like(ls_s)
        as_s[...] = jnp.zeros_like(as_s)

        def sel_body(t, carry, gl=gl, qg=qg):
            k0 = pl.multiple_of(t * SEL_TK, SEL_TK)
            s_all = _dot_nt(ks_s[pl.ds(k0, SEL_TK), gl], qg) * scale
            v_tt = jnp.concatenate([vst_s[t * (SEL_TK // KT) + u, gl, :] for u in range(SEL_TK // KT)], axis=1)
            picked = jnp.concatenate(
                [jnp.broadcast_to(sel_s[pl.ds(t * (SEL_TK // SEL_BLOCK) + u, 1), :], (SEL_BLOCK, QB))
                 for u in range(SEL_TK // SEL_BLOCK)], axis=0)
            kpos = k0 + lax.broadcasted_iota(jnp.int32, (SEL_TK, QB), 0)
            tq = qs + lax.broadcasted_iota(jnp.int32, (SEL_TK, QB), 1)
            _flash_step(s_all, jnp.where(kpos <= tq, picked, NEG_INF), v_tt, ms_s, ls_s, as_s)
            return carry

        lax.fori_loop(0, (qs + QB + SEL_TK - 1) // SEL_TK, sel_body, 0)

        mw_s[...] = jnp.full_like(mw_s, NEG_INF)
        lw_s[...] = jnp.zeros_like(lw_s)
        aw_s[...] = jnp.zeros_like(aw_s)

        def win_body(t, carry, gl=gl, qg=qg):
            k0 = pl.multiple_of((i - t) * WIN_TK, WIN_TK)
            s_all = _dot_nt(kw_s[pl.ds(k0, WIN_TK), gl], qg) * scale
            v_tt = jnp.concatenate([vwt_s[(i - t) * (WIN_TK // KT) + u, gl, :] for u in range(WIN_TK // KT)], axis=1)
            rel = (k0 + lax.broadcasted_iota(jnp.int32, (WIN_TK, QB), 0)
                   - (qs + lax.broadcasted_iota(jnp.int32, (WIN_TK, QB), 1)))
            ok = (rel <= 0) & (rel > -WINDOW)
            _flash_step(s_all, jnp.where(ok, 0.0, NEG_INF), v_tt, mw_s, lw_s, aw_s)
            return carry

        lax.fori_loop(0, jnp.minimum(i, WINDOW // WIN_TK) + 1, win_body, 0)

        for k in range(NSA_HPG):
            h = g * NSA_HPG + k
            cols = slice(k * QB, (k + 1) * QB)
            c0 = T_GATE + h * N_NSA_BRANCHES
            o_t = (gates_t[c0:c0 + 1, :] * oc_s[:, cols]
                   + (gates_t[c0 + 1:c0 + 2, :] / ls_s[:, cols]) * as_s[:, cols]
                   + (gates_t[c0 + 2:c0 + 3, :] / lw_s[:, cols]) * aw_s[:, cols])
            zz = z_ref[:, h * hd:(h + 1) * hd].astype(F32)
            o_ref[:, h * hd:(h + 1) * hd] = (o_t.T * _silu(zz)).astype(BF16)


def _nsa(proj, tail, kc, vc, qnw, ksnw, kwnw, selmap_t, batch, seq):
    nq = seq // QB
    hd = NSA_HEAD_DIM
    m = batch * seq

    def full(shape):
        return pl.BlockSpec(shape, lambda b, i: (0,) * len(shape))

    return pl.pallas_call(
        _nsa_kernel,
        grid=(batch, nq),
        in_specs=[
            pl.BlockSpec((QB, NSA_D), lambda b, i: (b * nq + i, C_Q // NSA_D)),
            pl.BlockSpec((seq, 2 * NSA_KV_D), lambda b, i: (b, C_KVSLC // (2 * NSA_KV_D))),
            pl.BlockSpec((seq, 2 * NSA_KV_D), lambda b, i: (b, C_KVWIN // (2 * NSA_KV_D))),
            pl.BlockSpec((1, N_CMP_PAD, NSA_KV_D), lambda b, i: (b, 0, 0)),
            pl.BlockSpec((1, NSA_KV_D, N_CMP_PAD), lambda b, i: (b, 0, 0)),
            pl.BlockSpec((QB, NSA_D), lambda b, i: (b * nq + i, C_ZNSA // NSA_D)),
            pl.BlockSpec((QB, LANES), lambda b, i: (b * nq + i, 0)),
            full((1, hd)), full((1, hd)), full((1, hd)),
            full((N_SLC_PAD, N_CMP_PAD)),
        ],
        out_specs=pl.BlockSpec((QB, NSA_D), lambda b, i: (b * nq + i, 0)),
        out_shape=jax.ShapeDtypeStruct((m, NSA_D), BF16),
        scratch_shapes=[
            pltpu.VMEM((seq, NSA_KV_D), BF16),
            pltpu.VMEM((seq, NSA_KV_D), BF16),
            pltpu.VMEM((seq // KT, NSA_KV_D, KT), BF16),
            pltpu.VMEM((seq // KT, NSA_KV_D, KT), BF16),
            pltpu.VMEM((NSA_KV_GROUPS, GROWS, hd), BF16),
            pltpu.VMEM((N_SLC, QB), F32),
            pltpu.VMEM((hd, GROWS), F32),
            pltpu.VMEM((1, GROWS), F32), pltpu.VMEM((1, GROWS), F32), pltpu.VMEM((hd, GROWS), F32),
            pltpu.VMEM((1, GROWS), F32), pltpu.VMEM((1, GROWS), F32), pltpu.VMEM((hd, GROWS), F32),
        ],
        compiler_params=_params(("arbitrary", "arbitrary")),
    )(proj, proj, proj, kc, vc, proj, tail, qnw, ksnw, kwnw, selmap_t)


OUT_TM = 1024
OUT_TN = 512


def _merge_kernel(ys_ref, yn_ref, gs_ref, gn_ref, ws_ref, wn_ref, o_ref):
    a = _dot(ys_ref[...], ws_ref[...])
    b = _dot(yn_ref[...], wn_ref[...])
    merged = _sigmoid(gs_ref[...].astype(F32)) * a + _sigmoid(gn_ref[...].astype(F32)) * b
    o_ref[...] = merged.astype(BF16)


def _merge(y_ssd, y_nsa, proj, w_s, w_n):
    m = y_ssd.shape[0]
    nb = D_MODEL // OUT_TN
    return pl.pallas_call(
        _merge_kernel,
        grid=(nb, m // OUT_TM),
        in_specs=[
            pl.BlockSpec((OUT_TM, D_MODEL), lambda j, i: (i, 0)),
            pl.BlockSpec((OUT_TM, D_MODEL), lambda j, i: (i, 0)),
            pl.BlockSpec((OUT_TM, OUT_TN), lambda j, i: (i, C_GLSSD // OUT_TN + j)),
            pl.BlockSpec((OUT_TM, OUT_TN), lambda j, i: (i, C_GLNSA // OUT_TN + j)),
            pl.BlockSpec((D_MODEL, OUT_TN), lambda j, i: (0, j)),
            pl.BlockSpec((D_MODEL, OUT_TN), lambda j, i: (0, j)),
        ],
        out_specs=pl.BlockSpec((OUT_TM, OUT_TN), lambda j, i: (i, j)),
        out_shape=jax.ShapeDtypeStruct((m, D_MODEL), BF16),
        compiler_params=_params(("arbitrary", "arbitrary")),
    )(y_ssd, y_nsa, proj, proj, w_s, w_n)


def _final_kernel(m_ref, x_ref, w_ref, o_ref):
    o_ref[...] = x_ref[...] + _dot(m_ref[...], w_ref[...])


def _final(merged, x2, w_o):
    m = merged.shape[0]
    nb = D_MODEL // OUT_TN
    return pl.pallas_call(
        _final_kernel,
        grid=(nb, m // OUT_TM),
        in_specs=[
            pl.BlockSpec((OUT_TM, D_MODEL), lambda j, i: (i, 0)),
            pl.BlockSpec((OUT_TM, OUT_TN), lambda j, i: (i, j)),
            pl.BlockSpec((D_MODEL, OUT_TN), lambda j, i: (0, j)),
        ],
        out_specs=pl.BlockSpec((OUT_TM, OUT_TN), lambda j, i: (i, j)),
        out_shape=jax.ShapeDtypeStruct((m, D_MODEL), F32),
        compiler_params=_params(("arbitrary", "arbitrary")),
    )(merged, x2, w_o)


def _w_in_sections(w_in):
    sec = {name: w_in[:, _OFFS[i]:_OFFS[i + 1]] for i, name in enumerate(
        ("z_ssd", "xbc", "dt", "q", "k_cmp", "v_cmp", "k_slc", "v_slc", "k_win", "v_win",
         "z_nsa", "gates", "gl_ssd", "gl_nsa"))}
    main = jnp.concatenate([
        sec["z_ssd"], sec["xbc"][:, :SSD_D_INNER], sec["q"], sec["z_nsa"], sec["gl_ssd"], sec["gl_nsa"],
        sec["xbc"][:, SSD_D_INNER:], sec["k_cmp"], sec["v_cmp"], sec["k_slc"], sec["v_slc"],
        sec["k_win"], sec["v_win"]], axis=1).astype(BF16)
    pad = LANES - SSD_HEADS - NSA_HEADS * N_NSA_BRANCHES
    tail = jnp.concatenate([sec["dt"], sec["gates"], jnp.zeros((D_MODEL, pad), w_in.dtype)], axis=1).astype(BF16)
    return main, tail


def _constants():
    tri = np.tril(np.ones((CQ, CQ), np.float32))
    expand = np.zeros((LANES, SSD_D_INNER), np.float32)
    for h in range(SSD_HEADS):
        expand[h, h * SSD_HEAD_DIM:(h + 1) * SSD_HEAD_DIM] = 1.0
    n_cmp = N_CMP_PAD - 1
    ci = np.arange(n_cmp)[:, None] * CMP_STRIDE
    sj = np.arange(32)[None, :] * SEL_BLOCK
    sel_map = ((ci < sj + SEL_BLOCK) & (ci + CMP_BLOCK > sj)).astype(np.float32)
    selmap_t = np.zeros((N_SLC_PAD, N_CMP_PAD), np.float32)
    selmap_t[:32, :n_cmp] = sel_map.T
    return jnp.asarray(tri), jnp.asarray(expand), jnp.asarray(selmap_t)


def _pad_lanes(v):
    return jnp.concatenate([v.astype(F32), jnp.zeros((LANES - v.shape[0],), F32)])[None, :]


def kernel(x, norm_w, w_in, conv_w, conv_b, dt_bias, a_log, d_skip, ssd_norm_w, q_norm_w, k_cmp_norm_w,
           k_slc_norm_w, k_win_norm_w, cmp_pe_k, cmp_w1_k, cmp_b1_k, cmp_w2_k, cmp_pe_v, cmp_w1_v, cmp_b1_v,
           cmp_w2_v, w_out_ssd, w_out_nsa, w_o):
    batch, seq, _ = x.shape
    assert seq == 2048 and (batch * seq) % IN_TM == 0
    x2 = x.reshape(batch * seq, D_MODEL)
    w_main, w_tail = _w_in_sections(w_in)
    tri, expand, selmap_t = _constants()

    proj, tail = _inproj(x2, norm_w[None, :], w_main, w_tail)

    y_ssd = _ssd(proj, tail,
                 conv_w[:, :SSD_D_INNER], conv_b[None, :SSD_D_INNER],
                 conv_w[:, SSD_D_INNER:], conv_b[None, SSD_D_INNER:],
                 _pad_lanes(dt_bias), _pad_lanes(a_log),
                 jnp.repeat(d_skip, SSD_HEAD_DIM)[None, :], ssd_norm_w[None, :],
                 tri, expand, batch, seq)

    kc, vc = _compress(proj, cmp_pe_k, cmp_w1_k.astype(BF16), cmp_b1_k[None, :], cmp_w2_k.astype(BF16),
                       cmp_pe_v, cmp_w1_v.astype(BF16), cmp_b1_v[None, :], cmp_w2_v.astype(BF16),
                       k_cmp_norm_w[None, :], batch, seq)

    y_nsa = _nsa(proj, tail, kc, vc, q_norm_w[None, :], k_slc_norm_w[None, :], k_win_norm_w[None, :],
                 selmap_t, batch, seq)

    merged = _merge(y_ssd, y_nsa, proj, w_out_ssd.astype(BF16), w_out_nsa.astype(BF16))
    out = _final(merged, x2, w_o.astype(BF16))
    return out.reshape(batch, seq, D_MODEL)
```

```python
import functools

import numpy as np
import jax
import jax.numpy as jnp
from jax import lax
from jax.experimental import pallas as pl
from jax.experimental.pallas import tpu as pltpu

D_MODEL = 2048
SSD_D_INNER = D_MODEL
SSD_HEAD_DIM = 64
SSD_HEADS = SSD_D_INNER // SSD_HEAD_DIM
SSD_GROUPS = 4
SSD_D_STATE = 128
SSD_CONV_K = 4
SSD_CHUNK = 128
SSD_BC_DIM = 2 * SSD_GROUPS * SSD_D_STATE
SSD_CONV_DIM = SSD_D_INNER + SSD_BC_DIM

NSA_HEADS = 16
NSA_HEAD_DIM = 128
NSA_KV_GROUPS = 2
NSA_HPG = NSA_HEADS // NSA_KV_GROUPS
NSA_D = NSA_HEADS * NSA_HEAD_DIM
NSA_KV_D = NSA_KV_GROUPS * NSA_HEAD_DIM
N_NSA_BRANCHES = 3
CMP_BLOCK = 32
CMP_STRIDE = 16
SEL_BLOCK = 64
SEL_TOPK = 8
WINDOW = 512
Q_BLOCK = 128

EPS = 1e-6
NEG_INF = -1e30
FORCED_SCORE = 1e9

LANES = 128
SUBLANES = 8
VMEM_LIMIT = 56 * 1024 * 1024

F32 = jnp.float32
BF16 = jnp.bfloat16
HIGHEST = lax.Precision.HIGHEST

C_ZSSD = 0
C_XS = 2048
C_Q = 4096
C_ZNSA = 6144
C_GLSSD = 8192
C_GLNSA = 10240
C_BC = 12288
C_KVCMP = 13312
C_KVSLC = 13824
C_KVWIN = 14336
N_MAIN = 14848
T_GATE = SSD_HEADS

_SIZES = [SSD_D_INNER, SSD_CONV_DIM, SSD_HEADS, NSA_D] + [NSA_KV_D] * 6 + [NSA_D, NSA_HEADS * N_NSA_BRANCHES, D_MODEL, D_MODEL]
_OFFS = [0] + [int(o) for o in np.cumsum(_SIZES)]


def _sigmoid(v):
    return 0.5 + 0.5 * jnp.tanh(0.5 * v)


def _silu(v):
    h = 0.5 * v
    return h + h * jnp.tanh(h)


def _softplus(v):
    return jnp.maximum(v, 0.0) + jnp.log(1.0 + jnp.exp(-jnp.abs(v)))


def _rms(v, w):
    ms = jnp.mean(v * v, axis=-1, keepdims=True)
    return v * lax.rsqrt(ms + EPS) * w


def _dot(a, b, precision=None):
    return jnp.dot(a, b, preferred_element_type=F32, precision=precision)


def _dot_nt(a, b, precision=None):
    return lax.dot_general(a, b, (((1,), (1,)), ((), ())), preferred_element_type=F32, precision=precision)


def _params(sem):
    return pltpu.CompilerParams(dimension_semantics=sem, vmem_limit_bytes=VMEM_LIMIT)


IN_TM = 1024
IN_TN = 512
IN_RC = 256


def _inproj_kernel(x_ref, nw_ref, w_ref, wt_ref, o_ref, t_ref, h_ref):
    j = pl.program_id(1)

    @pl.when(j == 0)
    def _():
        def body(r, carry):
            r0 = pl.multiple_of(r * IN_RC, IN_RC)
            h_ref[pl.ds(r0, IN_RC), :] = _rms(x_ref[pl.ds(r0, IN_RC), :], nw_ref[...]).astype(BF16)
            return carry

        lax.fori_loop(0, IN_TM // IN_RC, body, 0)
        t_ref[...] = _dot(h_ref[...], wt_ref[...])

    o_ref[...] = _dot(h_ref[...], w_ref[...]).astype(BF16)


def _inproj(x2, norm_w, w_main, w_tail):
    m = x2.shape[0]
    return pl.pallas_call(
        _inproj_kernel,
        grid=(m // IN_TM, N_MAIN // IN_TN),
        in_specs=[
            pl.BlockSpec((IN_TM, D_MODEL), lambda i, j: (i, 0)),
            pl.BlockSpec((1, D_MODEL), lambda i, j: (0, 0)),
            pl.BlockSpec((D_MODEL, IN_TN), lambda i, j: (0, j)),
            pl.BlockSpec((D_MODEL, LANES), lambda i, j: (0, 0)),
        ],
        out_specs=[
            pl.BlockSpec((IN_TM, IN_TN), lambda i, j: (i, j)),
            pl.BlockSpec((IN_TM, LANES), lambda i, j: (i, 0)),
        ],
        out_shape=[
            jax.ShapeDtypeStruct((m, N_MAIN), BF16),
            jax.ShapeDtypeStruct((m, LANES), F32),
        ],
        scratch_shapes=[pltpu.VMEM((IN_TM, D_MODEL), BF16)],
        compiler_params=_params(("arbitrary", "arbitrary")),
    )(x2, norm_w, w_main, w_tail)


CQ = SSD_CHUNK
PAIR = 2 * SSD_HEAD_DIM
N_PAIRS = SSD_HEADS // 2
PAIRS_PER_GROUP = N_PAIRS // SSD_GROUPS


def _ssd_kernel(z_ref, xs_ref, bc_ref, tail_ref, cwx_ref, cbx_ref, cwb_ref, cbb_ref, dtb_ref, alog_ref,
                dsk_ref, nw_ref, tri_ref, exp_ref, o_ref, xbuf, bbuf, xa_ref, ba_ref, st_ref, y_ref):
    c = pl.program_id(1)

    @pl.when(c == 0)
    def _():
        xbuf[0:SUBLANES, :] = jnp.zeros((SUBLANES, SSD_D_INNER), F32)
        bbuf[0:SUBLANES, :] = jnp.zeros((SUBLANES, SSD_BC_DIM), F32)
        st_ref[...] = jnp.zeros_like(st_ref)

    xbuf[SUBLANES:SUBLANES + CQ, :] = xs_ref[...].astype(F32)
    bbuf[SUBLANES:SUBLANES + CQ, :] = bc_ref[...].astype(F32)

    def conv(buf, w_ref, b_ref):
        acc = b_ref[...]
        for k in range(SSD_CONV_K):
            acc = acc + buf[pl.ds(SUBLANES - (SSD_CONV_K - 1) + k, CQ), :] * w_ref[k:k + 1, :]
        return _silu(acc)

    xa_ref[...] = conv(xbuf, cwx_ref, cbx_ref)
    ba_ref[...] = conv(bbuf, cwb_ref, cbb_ref)
    xbuf[0:SUBLANES, :] = xbuf[CQ:CQ + SUBLANES, :]
    bbuf[0:SUBLANES, :] = bbuf[CQ:CQ + SUBLANES, :]

    dt = _softplus(tail_ref[...] + dtb_ref[...])
    la = dt * (-jnp.exp(alog_ref[...]))
    lc = _dot(tri_ref[...], la, precision=HIGHEST)
    lc_t = lc.T
    dt_t = dt.T
    w_t = jnp.exp(lc_t[:, CQ - 1:CQ] - lc_t) * dt_t
    cd = jnp.exp(lc[CQ - 1:CQ, :])
    cdx = _dot(jnp.broadcast_to(cd, (SUBLANES, LANES)), exp_ref[...], precision=HIGHEST)

    row = lax.broadcasted_iota(jnp.int32, (CQ, CQ), 0)
    col = lax.broadcasted_iota(jnp.int32, (CQ, CQ), 1)
    causal = row >= col
    lo_half = col < SSD_HEAD_DIM

    for g in range(SSD_GROUPS):
        b_f = ba_ref[:, g * SSD_D_STATE:(g + 1) * SSD_D_STATE]
        c_f = ba_ref[:, (SSD_GROUPS + g) * SSD_D_STATE:(SSD_GROUPS + g + 1) * SSD_D_STATE]
        cb = _dot_nt(c_f.astype(BF16), b_f.astype(BF16))
        b_t = b_f.T
        for pp in range(PAIRS_PER_GROUP):
            p = g * PAIRS_PER_GROUP + pp
            lanes = slice(p * PAIR, (p + 1) * PAIR)
            xp = xa_ref[:, lanes].astype(BF16)
            s_prev = st_ref[:, lanes]
            s_prev_b = s_prev.astype(BF16)
            ys, sn = [], []
            for hh in range(2):
                h = 2 * p + hh
                lcol = jnp.broadcast_to(lc[:, h:h + 1], (CQ, CQ))
                lmat = jnp.exp(jnp.where(causal, lcol - lc_t[h:h + 1, :], -jnp.inf))
                m_h = (cb * lmat * dt_t[h:h + 1, :]).astype(BF16)
                c_s = (c_f * jnp.exp(lcol)).astype(BF16)
                ys.append(_dot(m_h, xp) + _dot(c_s, s_prev_b))
                w_h = (b_t * w_t[h:h + 1, :]).astype(BF16)
                sn.append(_dot(w_h, xp))
            y_ref[:, lanes] = jnp.where(lo_half, ys[0], ys[1])
            st_ref[:, lanes] = s_prev * cdx[0:1, lanes] + jnp.where(lo_half, sn[0], sn[1])

    y = y_ref[...] + xa_ref[...] * dsk_ref[...]
    y = y * _silu(z_ref[...].astype(F32))
    o_ref[...] = _rms(y, nw_ref[...]).astype(BF16)


def _ssd(proj, tail, cwx, cbx, cwb, cbb, dtb, alog, dsk, nw, tri, expand, batch, seq):
    nc = seq // CQ
    m = batch * seq

    def full(shape):
        return pl.BlockSpec(shape, lambda b, c: (0,) * len(shape))

    return pl.pallas_call(
        _ssd_kernel,
        grid=(batch, nc),
        in_specs=[
            pl.BlockSpec((CQ, SSD_D_INNER), lambda b, c: (b * nc + c, C_ZSSD // SSD_D_INNER)),
            pl.BlockSpec((CQ, SSD_D_INNER), lambda b, c: (b * nc + c, C_XS // SSD_D_INNER)),
            pl.BlockSpec((CQ, SSD_BC_DIM), lambda b, c: (b * nc + c, C_BC // SSD_BC_DIM)),
            pl.BlockSpec((CQ, LANES), lambda b, c: (b * nc + c, 0)),
            full((SSD_CONV_K, SSD_D_INNER)), full((1, SSD_D_INNER)),
            full((SSD_CONV_K, SSD_BC_DIM)), full((1, SSD_BC_DIM)),
            full((1, LANES)), full((1, LANES)),
            full((1, SSD_D_INNER)), full((1, SSD_D_INNER)),
            full((CQ, CQ)), full((LANES, SSD_D_INNER)),
        ],
        out_specs=pl.BlockSpec((CQ, SSD_D_INNER), lambda b, c: (b * nc + c, 0)),
        out_shape=jax.ShapeDtypeStruct((m, SSD_D_INNER), BF16),
        scratch_shapes=[
            pltpu.VMEM((CQ + SUBLANES, SSD_D_INNER), F32),
            pltpu.VMEM((CQ + SUBLANES, SSD_BC_DIM), F32),
            pltpu.VMEM((CQ, SSD_D_INNER), F32),
            pltpu.VMEM((CQ, SSD_BC_DIM), F32),
            pltpu.VMEM((SSD_D_STATE, SSD_D_INNER), F32),
            pltpu.VMEM((CQ, SSD_D_INNER), F32),
        ],
        compiler_params=_params(("arbitrary", "arbitrary")),
    )(proj, proj, proj, tail, cwx, cbx, cwb, cbb, dtb, alog, dsk, nw, tri, expand)


N_CMP_PAD = 128
CMP_HALF = CMP_BLOCK // CMP_STRIDE


def _cmp_kernel(kv_ref, pek_ref, w1k_ref, b1k_ref, w2k_ref, pev_ref, w1v_ref, b1v_ref, w2v_ref, knw_ref,
                kc_ref, vc_ref, buf):
    hd = NSA_HEAD_DIM
    for slab in range(2 * NSA_KV_GROUPS):
        buf[slab] = kv_ref[:, slab * hd:(slab + 1) * hd].astype(F32)
    for kind, (pe_ref, w1_ref, b1_ref, w2_ref, out_ref) in enumerate((
            (pek_ref, w1k_ref, b1k_ref, w2k_ref, kc_ref), (pev_ref, w1v_ref, b1v_ref, w2v_ref, vc_ref))):
        for g in range(NSA_KV_GROUPS):
            slab = kind * NSA_KV_GROUPS + g
            u = [jnp.zeros((N_CMP_PAD, hd), F32) for _ in range(CMP_HALF)]
            for r in range(CMP_STRIDE):
                xr = buf[slab, pl.ds(r, N_CMP_PAD, stride=CMP_STRIDE), :]
                for a in range(CMP_HALF):
                    l = a * CMP_STRIDE + r
                    u[a] = u[a] + _dot((xr + pe_ref[l:l + 1, :]).astype(BF16), w1_ref[l * hd:(l + 1) * hd, :])
            acc = u[0] + pltpu.roll(u[1], N_CMP_PAD - 1, 0) + b1_ref[...]
            out = _dot(_silu(acc).astype(BF16), w2_ref[...])
            if kind == 0:
                out_ref[0, :, g * hd:(g + 1) * hd] = _rms(out, knw_ref[...]).astype(BF16)
            else:
                out_ref[0, g * hd:(g + 1) * hd, :] = out.T.astype(BF16)


def _compress(proj, pek, w1k, b1k, w2k, pev, w1v, b1v, w2v, knw, batch, seq):
    hd = NSA_HEAD_DIM

    def full(shape):
        return pl.BlockSpec(shape, lambda b: (0,) * len(shape))

    return pl.pallas_call(
        _cmp_kernel,
        grid=(batch,),
        in_specs=[
            pl.BlockSpec((seq, 2 * NSA_KV_D), lambda b: (b, C_KVCMP // (2 * NSA_KV_D))),
            full((CMP_BLOCK, hd)), full((CMP_BLOCK * hd, hd)), full((1, hd)), full((hd, hd)),
            full((CMP_BLOCK, hd)), full((CMP_BLOCK * hd, hd)), full((1, hd)), full((hd, hd)),
            full((1, hd)),
        ],
        out_specs=[pl.BlockSpec((1, N_CMP_PAD, NSA_KV_D), lambda b: (b, 0, 0)),
                   pl.BlockSpec((1, NSA_KV_D, N_CMP_PAD), lambda b: (b, 0, 0))],
        out_shape=[jax.ShapeDtypeStruct((batch, N_CMP_PAD, NSA_KV_D), BF16),
                   jax.ShapeDtypeStruct((batch, NSA_KV_D, N_CMP_PAD), BF16)],
        scratch_shapes=[pltpu.VMEM((2 * NSA_KV_GROUPS, seq, hd), F32)],
        compiler_params=_params(("arbitrary",)),
    )(proj, pek, w1k, b1k, w2k, pev, w1v, b1v, w2v, knw)


QB = Q_BLOCK
GROWS = NSA_HPG * QB
TK = 256
BR_SEL, BR_WIN = 0, 1
N_SLC = 32
N_SLC_PAD = 128
NORM_RC = 256
KT = 128
PW = 2 * QB
N_HPAIRS = NSA_HPG // 2
KX = 2 * NSA_HEAD_DIM
LOG2E = 1.4426950408889634


def _nsa_kernel(q_ref, kvs_ref, kvw_ref, kc_ref, vct_ref, z_ref, tail_ref, qnw_ref, ksnw_ref, kwnw_ref, selmap_ref,
                o_ref, kx_s, vt_s, qg_s, oc_s, m_s, l_s, acc_s, s_s, p_s):
    i = pl.program_id(1)
    qs = i * QB
    hd = NSA_HEAD_DIM
    seq = kvs_ref.shape[0]

    @pl.when(i == 0)
    def _():
        for br, src, nw in ((BR_SEL, kvs_ref, ksnw_ref), (BR_WIN, kvw_ref, kwnw_ref)):
            for g in range(NSA_KV_GROUPS):
                def body(r, carry, br=br, src=src, nw=nw, g=g):
                    r0 = pl.multiple_of(r * NORM_RC, NORM_RC)
                    kk = src[pl.ds(r0, NORM_RC), g * hd:(g + 1) * hd].astype(F32)
                    kx_s[br, pl.ds(r0, NORM_RC), g * KX:g * KX + hd] = _rms(kk, nw[...]).astype(BF16)
                    blk = (r0 + lax.broadcasted_iota(jnp.int32, (NORM_RC, hd), 0)) >> 6
                    hit = blk == lax.broadcasted_iota(jnp.int32, (NORM_RC, hd), 1)
                    aux = jnp.where(hit, 1.0 if br == BR_SEL else 0.0, 0.0)
                    kx_s[br, pl.ds(r0, NORM_RC), g * KX + hd:(g + 1) * KX] = aux.astype(BF16)
                    return carry

                lax.fori_loop(0, seq // NORM_RC, body, 0)

                def body_t(r, carry, br=br, src=src, g=g):
                    r0 = pl.multiple_of(r * KT, KT)
                    vv = src[pl.ds(r0, KT), NSA_KV_D + g * hd:NSA_KV_D + (g + 1) * hd].astype(F32)
                    vt_s[br, r, g * hd:(g + 1) * hd, :] = vv.T.astype(BF16)
                    return carry

                lax.fori_loop(0, seq // KT, body_t, 0)

    for h in range(NSA_HEADS):
        g, k = divmod(h, NSA_HPG)
        qh = q_ref[:, h * hd:(h + 1) * hd].astype(F32)
        qg_s[g, k * QB:(k + 1) * QB, 0:hd] = (_rms(qh, qnw_ref[...]) * (hd ** -0.5 * LOG2E)).astype(BF16)

    gates_t = _sigmoid(tail_ref[...]).T

    n_sel = (qs + QB + TK - 1) // TK
    win_first = jnp.maximum(qs - (WINDOW - 1), 0) // TK
    n_tiles = 2 * n_sel - win_first

    def tile_of(u):
        is_win = u >= n_sel
        return jnp.where(is_win, BR_WIN, BR_SEL), jnp.where(is_win, win_first + (u - n_sel), u)

    rel0 = (lax.broadcasted_iota(jnp.int32, (TK, QB), 0) - lax.broadcasted_iota(jnp.int32, (TK, QB), 1)) - qs

    c_idx = lax.broadcasted_iota(jnp.int32, (N_CMP_PAD, GROWS), 0)
    q_in = lax.broadcasted_iota(jnp.int32, (N_CMP_PAD, GROWS), 1) & (QB - 1)
    mask_c = c_idx * CMP_STRIDE + (CMP_BLOCK - 1) <= qs + q_in

    jj = lax.broadcasted_iota(jnp.int32, (N_SLC, QB), 0)
    tq_t = qs + lax.broadcasted_iota(jnp.int32, (N_SLC, QB), 1)

    for g in range(NSA_KV_GROUPS):
        gl = slice(g * hd, (g + 1) * hd)
        s = _dot_nt(kc_ref[0, :, gl], qg_s[g, :, 0:hd])
        s = jnp.where(mask_c, s, NEG_INF)
        e = jnp.exp2(s - jnp.max(s, axis=0, keepdims=True))
        p_cmp = jnp.where(mask_c, e / jnp.sum(e, axis=0, keepdims=True), 0.0)
        oc_s[...] = _dot(vct_ref[0, gl, :], p_cmp.astype(BF16))

        p_sum = p_cmp[:, 0:QB]
        for k in range(1, NSA_HPG):
            p_sum = p_sum + p_cmp[:, k * QB:(k + 1) * QB]
        imp = _dot(selmap_ref[...], p_sum, precision=HIGHEST)[0:N_SLC, :]
        forced = (jj == (tq_t >> 6)) | (jj == 0)
        imp = jnp.where(forced, FORCED_SCORE, jnp.where(jj * SEL_BLOCK <= tq_t, imp, -1.0))
        rank = jnp.zeros((N_SLC, QB), jnp.int32)
        for j2 in range(N_SLC):
            other = imp[j2:j2 + 1, :]
            beats = (other > imp) | ((other == imp) & (jj > j2))
            rank = rank + jnp.where(beats, 1, 0)
        picked = (rank < SEL_TOPK) & (jj * SEL_BLOCK <= tq_t)
        sel_bias = jnp.concatenate([jnp.where(picked, 0.0, NEG_INF).astype(F32),
                                    jnp.zeros((hd - N_SLC, QB), F32)], axis=0)
        sel_bias_q = sel_bias.T.astype(BF16)
        for k in range(NSA_HPG):
            qg_s[g, k * QB:(k + 1) * QB, hd:KX] = sel_bias_q

        m_s[...] = jnp.full_like(m_s, NEG_INF)
        l_s[...] = jnp.zeros_like(l_s)
        acc_s[...] = jnp.zeros_like(acc_s)

        def keys_of(br, t, g=g):
            return kx_s[br, pl.ds(pl.multiple_of(t * TK, TK), TK), g * KX:(g + 1) * KX]

        def finish(br, t, alphas, gl=gl):
            v_tt = jnp.concatenate([vt_s[br, t * (TK // KT) + u, gl, :] for u in range(TK // KT)], axis=1)
            for pr in range(N_HPAIRS):
                cols = slice(pr * PW, (pr + 1) * PW)
                acc_s[br, :, cols] = alphas[pr] * acc_s[br, :, cols] + _dot(v_tt, p_s[:, cols])

        def body(u, carry, g=g):
            alphas_prev, br_prev, t_prev = carry
            br, t = tile_of(u)
            finish(br_prev, t_prev, alphas_prev)
            k_next = keys_of(*tile_of(jnp.minimum(u + 1, n_tiles - 1)))
            rel = rel0 + t * TK
            bias = jnp.where((rel <= 0) & (rel > jnp.where(br == BR_WIN, -WINDOW, -(1 << 30))), 0.0, NEG_INF)
            bias = jnp.concatenate([bias, bias], axis=1)
            alphas = []
            for pr in range(N_HPAIRS):
                cols = slice(pr * PW, (pr + 1) * PW)
                s2 = s_s[:, cols] + bias
                s_s[:, cols] = _dot_nt(k_next, qg_s[g, cols, :])
                m_prev = m_s[br, :, cols]
                m_new = jnp.maximum(m_prev, jnp.max(s2, axis=0, keepdims=True))
                alpha = jnp.exp2(m_prev - m_new)
                p = jnp.exp2(s2 - m_new)
                l_s[br, :, cols] = alpha * l_s[br, :, cols] + jnp.sum(p, axis=0, keepdims=True)
                m_s[br, :, cols] = m_new
                p_s[:, cols] = p.astype(BF16)
                alphas.append(alpha)
            return tuple(alphas), br, t

        k_first = keys_of(*tile_of(0))
        for pr in range(N_HPAIRS):
            s_s[:, pr * PW:(pr + 1) * PW] = _dot_nt(k_first, qg_s[g, pr * PW:(pr + 1) * PW, :])
        p_s[...] = jnp.zeros_like(p_s)
        carry0 = (tuple(jnp.ones((1, PW), F32) for _ in range(N_HPAIRS)), jnp.int32(BR_SEL), jnp.int32(0))
        alphas_last, br_last, t_last = lax.fori_loop(0, n_tiles, body, carry0)
        finish(br_last, t_last, alphas_last)

        for pr in range(N_HPAIRS):
            cols = slice(pr * PW, (pr + 1) * PW)
            heads = [g * NSA_HPG + 2 * pr + u for u in range(2)]

            def gate_row(branch):
                return jnp.concatenate([gates_t[T_GATE + h * N_NSA_BRANCHES + branch:
                                                T_GATE + h * N_NSA_BRANCHES + branch + 1, :] for h in heads], axis=1)

            o_t = (gate_row(0) * oc_s[:, cols]
                   + (gate_row(1) / l_s[BR_SEL, :, cols]) * acc_s[BR_SEL, :, cols]
                   + (gate_row(2) / l_s[BR_WIN, :, cols]) * acc_s[BR_WIN, :, cols])
            for u, h in enumerate(heads):
                zz = z_ref[:, h * hd:(h + 1) * hd].astype(F32)
                o_ref[:, h * hd:(h + 1) * hd] = (o_t[:, u * QB:(u + 1) * QB].T * _silu(zz)).astype(BF16)


def _nsa(proj, tail, kc, vc, qnw, ksnw, kwnw, selmap_t, batch, seq):
    nq = seq // QB
    hd = NSA_HEAD_DIM
    m = batch * seq

    def full(shape):
        return pl.BlockSpec(shape, lambda b, i: (0,) * len(shape))

    return pl.pallas_call(
        _nsa_kernel,
        grid=(batch, nq),
        in_specs=[
            pl.BlockSpec((QB, NSA_D), lambda b, i: (b * nq + i, C_Q // NSA_D)),
            pl.BlockSpec((seq, 2 * NSA_KV_D), lambda b, i: (b, C_KVSLC // (2 * NSA_KV_D))),
            pl.BlockSpec((seq, 2 * NSA_KV_D), lambda b, i: (b, C_KVWIN // (2 * NSA_KV_D))),
            pl.BlockSpec((1, N_CMP_PAD, NSA_KV_D), lambda b, i: (b, 0, 0)),
            pl.BlockSpec((1, NSA_KV_D, N_CMP_PAD), lambda b, i: (b, 0, 0)),
            pl.BlockSpec((QB, NSA_D), lambda b, i: (b * nq + i, C_ZNSA // NSA_D)),
            pl.BlockSpec((QB, LANES), lambda b, i: (b * nq + i, 0)),
            full((1, hd)), full((1, hd)), full((1, hd)),
            full((N_SLC_PAD, N_CMP_PAD)),
        ],
        out_specs=pl.BlockSpec((QB, NSA_D), lambda b, i: (b * nq + i, 0)),
        out_shape=jax.ShapeDtypeStruct((m, NSA_D), BF16),
        scratch_shapes=[
            pltpu.VMEM((2, seq, NSA_KV_GROUPS * KX), BF16),
            pltpu.VMEM((2, seq // KT, NSA_KV_D, KT), BF16),
            pltpu.VMEM((NSA_KV_GROUPS, GROWS, KX), BF16),
            pltpu.VMEM((hd, GROWS), F32),
            pltpu.VMEM((2, 1, GROWS), F32),
            pltpu.VMEM((2, 1, GROWS), F32),
            pltpu.VMEM((2, hd, GROWS), F32),
            pltpu.VMEM((TK, GROWS), F32),
            pltpu.VMEM((TK, GROWS), BF16),
        ],
        compiler_params=_params(("arbitrary", "arbitrary")),
    )(proj, proj, proj, kc, vc, proj, tail, qnw, ksnw, kwnw, selmap_t)


OUT_TM = 1024
OUT_TN = 512


def _merge_kernel(ys_ref, yn_ref, gs_ref, gn_ref, ws_ref, wn_ref, o_ref):
    a = _dot(ys_ref[...], ws_ref[...])
    b = _dot(yn_ref[...], wn_ref[...])
    merged = _sigmoid(gs_ref[...].astype(F32)) * a + _sigmoid(gn_ref[...].astype(F32)) * b
    o_ref[...] = merged.astype(BF16)


def _merge(y_ssd, y_nsa, proj, w_s, w_n):
    m = y_ssd.shape[0]
    nb = D_MODEL // OUT_TN
    return pl.pallas_call(
        _merge_kernel,
        grid=(nb, m // OUT_TM),
        in_specs=[
            pl.BlockSpec((OUT_TM, D_MODEL), lambda j, i: (i, 0)),
            pl.BlockSpec((OUT_TM, D_MODEL), lambda j, i: (i, 0)),
            pl.BlockSpec((OUT_TM, OUT_TN), lambda j, i: (i, C_GLSSD // OUT_TN + j)),
            pl.BlockSpec((OUT_TM, OUT_TN), lambda j, i: (i, C_GLNSA // OUT_TN + j)),
            pl.BlockSpec((D_MODEL, OUT_TN), lambda j, i: (0, j)),
            pl.BlockSpec((D_MODEL, OUT_TN), lambda j, i: (0, j)),
        ],
        out_specs=pl.BlockSpec((OUT_TM, OUT_TN), lambda j, i: (i, j)),
        out_shape=jax.ShapeDtypeStruct((m, D_MODEL), BF16),
        compiler_params=_params(("arbitrary", "arbitrary")),
    )(y_ssd, y_nsa, proj, proj, w_s, w_n)


def _final_kernel(m_ref, x_ref, w_ref, o_ref):
    o_ref[...] = x_ref[...] + _dot(m_ref[...], w_ref[...])


def _final(merged, x2, w_o):
    m = merged.shape[0]
    nb = D_MODEL // OUT_TN
    return pl.pallas_call(
        _final_kernel,
        grid=(nb, m // OUT_TM),
        in_specs=[
            pl.BlockSpec((OUT_TM, D_MODEL), lambda j, i: (i, 0)),
            pl.BlockSpec((OUT_TM, OUT_TN), lambda j, i: (i, j)),
            pl.BlockSpec((D_MODEL, OUT_TN), lambda j, i: (0, j)),
        ],
        out_specs=pl.BlockSpec((OUT_TM, OUT_TN), lambda j, i: (i, j)),
        out_shape=jax.ShapeDtypeStruct((m, D_MODEL), F32),
        compiler_params=_params(("arbitrary", "arbitrary")),
    )(merged, x2, w_o)


def _w_in_sections(w_in):
    sec = {name: w_in[:, _OFFS[i]:_OFFS[i + 1]] for i, name in enumerate(
        ("z_ssd", "xbc", "dt", "q", "k_cmp", "v_cmp", "k_slc", "v_slc", "k_win", "v_win",
         "z_nsa", "gates", "gl_ssd", "gl_nsa"))}
    main = jnp.concatenate([
        sec["z_ssd"], sec["xbc"][:, :SSD_D_INNER], sec["q"], sec["z_nsa"], sec["gl_ssd"], sec["gl_nsa"],
        sec["xbc"][:, SSD_D_INNER:], sec["k_cmp"], sec["v_cmp"], sec["k_slc"], sec["v_slc"],
        sec["k_win"], sec["v_win"]], axis=1).astype(BF16)
    pad = LANES - SSD_HEADS - NSA_HEADS * N_NSA_BRANCHES
    tail = jnp.concatenate([sec["dt"], sec["gates"], jnp.zeros((D_MODEL, pad), w_in.dtype)], axis=1).astype(BF16)
    return main, tail


def _constants():
    tri = np.tril(np.ones((CQ, CQ), np.float32))
    expand = np.zeros((LANES, SSD_D_INNER), np.float32)
    for h in range(SSD_HEADS):
        expand[h, h * SSD_HEAD_DIM:(h + 1) * SSD_HEAD_DIM] = 1.0
    n_cmp = N_CMP_PAD - 1
    ci = np.arange(n_cmp)[:, None] * CMP_STRIDE
    sj = np.arange(32)[None, :] * SEL_BLOCK
    sel_map = ((ci < sj + SEL_BLOCK) & (ci + CMP_BLOCK > sj)).astype(np.float32)
    selmap_t = np.zeros((N_SLC_PAD, N_CMP_PAD), np.float32)
    selmap_t[:32, :n_cmp] = sel_map.T
    return jnp.asarray(tri), jnp.asarray(expand), jnp.asarray(selmap_t)


def _pad_lanes(v):
    return jnp.concatenate([v.astype(F32), jnp.zeros((LANES - v.shape[0],), F32)])[None, :]


def kernel(x, norm_w, w_in, conv_w, conv_b, dt_bias, a_log, d_skip, ssd_norm_w, q_norm_w, k_cmp_norm_w,
           k_slc_norm_w, k_win_norm_w, cmp_pe_k, cmp_w1_k, cmp_b1_k, cmp_w2_k, cmp_pe_v, cmp_w1_v, cmp_b1_v,
           cmp_w2_v, w_out_ssd, w_out_nsa, w_o):
    batch, seq, _ = x.shape
    assert seq == 2048 and (batch * seq) % IN_TM == 0
    x2 = x.reshape(batch * seq, D_MODEL)
    w_main, w_tail = _w_in_sections(w_in)
    tri, expand, selmap_t = _constants()

    proj, tail = _inproj(x2, norm_w[None, :], w_main, w_tail)

    y_ssd = _ssd(proj, tail,
                 conv_w[:, :SSD_D_INNER], conv_b[None, :SSD_D_INNER],
                 conv_w[:, SSD_D_INNER:], conv_b[None, SSD_D_INNER:],
                 _pad_lanes(dt_bias), _pad_lanes(a_log),
                 jnp.repeat(d_skip, SSD_HEAD_DIM)[None, :], ssd_norm_w[None, :],
                 tri, expand, batch, seq)

    kc, vc = _compress(proj, cmp_pe_k, cmp_w1_k.astype(BF16), cmp_b1_k[None, :], cmp_w2_k.astype(BF16),
                       cmp_pe_v, cmp_w1_v.astype(BF16), cmp_b1_v[None, :], cmp_w2_v.astype(BF16),
                       k_cmp_norm_w[None, :], batch, seq)

    y_nsa = _nsa(proj, tail, kc, vc, q_norm_w[None, :], k_slc_norm_w[None, :], k_win_norm_w[None, :],
                 selmap_t, batch, seq)

    merged = _merge(y_ssd, y_nsa, proj, w_out_ssd.astype(BF16), w_out_nsa.astype(BF16))
    out = _final(merged, x2, w_o.astype(BF16))
    return out.reshape(batch, seq, D_MODEL)
```

```python
import functools

import numpy as np
import jax
import jax.numpy as jnp
from jax import lax
from jax.experimental import pallas as pl
from jax.experimental.pallas import tpu as pltpu

D_MODEL = 2048
SSD_D_INNER = D_MODEL
SSD_HEAD_DIM = 64
SSD_HEADS = SSD_D_INNER // SSD_HEAD_DIM
SSD_GROUPS = 4
SSD_D_STATE = 128
SSD_CONV_K = 4
SSD_CHUNK = 128
SSD_BC_DIM = 2 * SSD_GROUPS * SSD_D_STATE
SSD_CONV_DIM = SSD_D_INNER + SSD_BC_DIM

NSA_HEADS = 16
NSA_HEAD_DIM = 128
NSA_KV_GROUPS = 2
NSA_HPG = NSA_HEADS // NSA_KV_GROUPS
NSA_D = NSA_HEADS * NSA_HEAD_DIM
NSA_KV_D = NSA_KV_GROUPS * NSA_HEAD_DIM
N_NSA_BRANCHES = 3
CMP_BLOCK = 32
CMP_STRIDE = 16
SEL_BLOCK = 64
SEL_TOPK = 8
WINDOW = 512
Q_BLOCK = 128

EPS = 1e-6
NEG_INF = -1e30
FORCED_SCORE = 1e9

LANES = 128
SUBLANES = 8
VMEM_LIMIT = 56 * 1024 * 1024

F32 = jnp.float32
BF16 = jnp.bfloat16
HIGHEST = lax.Precision.HIGHEST

C_ZSSD = 0
C_XS = 2048
C_Q = 4096
C_ZNSA = 6144
C_GLSSD = 8192
C_GLNSA = 10240
C_BC = 12288
C_KVCMP = 13312
C_KVSLC = 13824
C_KVWIN = 14336
N_MAIN = 14848
T_GATE = SSD_HEADS

_SIZES = [SSD_D_INNER, SSD_CONV_DIM, SSD_HEADS, NSA_D] + [NSA_KV_D] * 6 + [NSA_D, NSA_HEADS * N_NSA_BRANCHES, D_MODEL, D_MODEL]
_OFFS = [0] + [int(o) for o in np.cumsum(_SIZES)]


def _sigmoid(v):
    return 0.5 + 0.5 * jnp.tanh(0.5 * v)


def _silu(v):
    h = 0.5 * v
    return h + h * jnp.tanh(h)


def _softplus(v):
    return jnp.maximum(v, 0.0) + jnp.log(1.0 + jnp.exp(-jnp.abs(v)))


def _rms(v, w):
    ms = jnp.mean(v * v, axis=-1, keepdims=True)
    return v * lax.rsqrt(ms + EPS) * w


def _dot(a, b, precision=None):
    return jnp.dot(a, b, preferred_element_type=F32, precision=precision)


def _dot_nt(a, b, precision=None):
    return lax.dot_general(a, b, (((1,), (1,)), ((), ())), preferred_element_type=F32, precision=precision)


def _params(sem):
    return pltpu.CompilerParams(dimension_semantics=sem, vmem_limit_bytes=VMEM_LIMIT)


IN_TM = 1024
IN_TN = 512
IN_RC = 256


def _inproj_kernel(x_ref, nw_ref, w_ref, wt_ref, o_ref, t_ref, h_ref):
    j = pl.program_id(1)

    @pl.when(j == 0)
    def _():
        def body(r, carry):
            r0 = pl.multiple_of(r * IN_RC, IN_RC)
            h_ref[pl.ds(r0, IN_RC), :] = _rms(x_ref[pl.ds(r0, IN_RC), :], nw_ref[...]).astype(BF16)
            return carry

        lax.fori_loop(0, IN_TM // IN_RC, body, 0)
        t_ref[...] = _dot(h_ref[...], wt_ref[...])

    o_ref[...] = _dot(h_ref[...], w_ref[...]).astype(BF16)


def _inproj(x2, norm_w, w_main, w_tail):
    m = x2.shape[0]
    return pl.pallas_call(
        _inproj_kernel,
        grid=(m // IN_TM, N_MAIN // IN_TN),
        in_specs=[
            pl.BlockSpec((IN_TM, D_MODEL), lambda i, j: (i, 0)),
            pl.BlockSpec((1, D_MODEL), lambda i, j: (0, 0)),
            pl.BlockSpec((D_MODEL, IN_TN), lambda i, j: (0, j)),
            pl.BlockSpec((D_MODEL, LANES), lambda i, j: (0, 0)),
        ],
        out_specs=[
            pl.BlockSpec((IN_TM, IN_TN), lambda i, j: (i, j)),
            pl.BlockSpec((IN_TM, LANES), lambda i, j: (i, 0)),
        ],
        out_shape=[
            jax.ShapeDtypeStruct((m, N_MAIN), BF16),
            jax.ShapeDtypeStruct((m, LANES), F32),
        ],
        scratch_shapes=[pltpu.VMEM((IN_TM, D_MODEL), BF16)],
        compiler_params=_params(("arbitrary", "arbitrary")),
    )(x2, norm_w, w_main, w_tail)


CQ = SSD_CHUNK
PAIR = 2 * SSD_HEAD_DIM
N_PAIRS = SSD_HEADS // 2
PAIRS_PER_GROUP = N_PAIRS // SSD_GROUPS
CONV_HIST = 16
CONV_COLS = 256


def _ssd_kernel(z_ref, xs_ref, bc_ref, tail_ref, cwx_ref, cbx_ref, cwb_ref, cbb_ref, dtb_ref, alog_ref,
                dsk_ref, nw_ref, tri_ref, exp_ref, shift_ref, o_ref, xbuf, bbuf, xa_ref, ba_ref, st_ref, y_ref):
    c = pl.program_id(1)

    @pl.when(c == 0)
    def _():
        xbuf[0:CONV_HIST, :] = jnp.zeros((CONV_HIST, SSD_D_INNER), BF16)
        bbuf[0:CONV_HIST, :] = jnp.zeros((CONV_HIST, SSD_BC_DIM), BF16)
        st_ref[...] = jnp.zeros_like(st_ref)

    xbuf[CONV_HIST:CONV_HIST + CQ, :] = xs_ref[...]
    bbuf[CONV_HIST:CONV_HIST + CQ, :] = bc_ref[...]

    def conv(buf, cur_ref, w_ref, b_ref, out_ref):
        for j in range(buf.shape[1] // CONV_COLS):
            cs = slice(j * CONV_COLS, (j + 1) * CONV_COLS)
            shifted = _dot(shift_ref[...], buf[:, cs])
            acc = b_ref[:, cs] + cur_ref[:, cs].astype(F32) * w_ref[SSD_CONV_K - 1:SSD_CONV_K, cs]
            for k in range(SSD_CONV_K - 1):
                acc = acc + shifted[k * CQ:(k + 1) * CQ, :] * w_ref[k:k + 1, cs]
            out_ref[:, cs] = _silu(acc)

    conv(xbuf, xs_ref, cwx_ref, cbx_ref, xa_ref)
    conv(bbuf, bc_ref, cwb_ref, cbb_ref, ba_ref)
    xbuf[0:CONV_HIST, :] = xbuf[CQ:CQ + CONV_HIST, :]
    bbuf[0:CONV_HIST, :] = bbuf[CQ:CQ + CONV_HIST, :]

    dt = _softplus(tail_ref[...] + dtb_ref[...])
    la = dt * (-jnp.exp(alog_ref[...]))
    lc = _dot(tri_ref[...], la, precision=HIGHEST) * LOG2E
    lc_t = lc.T
    dt_t = dt.T
    w_t = jnp.exp2(lc_t[:, CQ - 1:CQ] - lc_t) * dt_t
    cd = jnp.exp2(lc[CQ - 1:CQ, :])
    cdx = _dot(jnp.broadcast_to(cd, (SUBLANES, LANES)), exp_ref[...], precision=HIGHEST)

    row = lax.broadcasted_iota(jnp.int32, (CQ, CQ), 0)
    col = lax.broadcasted_iota(jnp.int32, (CQ, CQ), 1)
    causal = row >= col
    lo_half = col < SSD_HEAD_DIM

    for g in range(SSD_GROUPS):
        b_f = ba_ref[:, g * SSD_D_STATE:(g + 1) * SSD_D_STATE]
        c_f = ba_ref[:, (SSD_GROUPS + g) * SSD_D_STATE:(SSD_GROUPS + g + 1) * SSD_D_STATE]
        cb = _dot_nt(c_f.astype(BF16), b_f.astype(BF16))
        b_t = b_f.T
        for pp in range(PAIRS_PER_GROUP):
            p = g * PAIRS_PER_GROUP + pp
            lanes = slice(p * PAIR, (p + 1) * PAIR)
            xp = xa_ref[:, lanes].astype(BF16)
            s_prev = st_ref[:, lanes]
            s_prev_b = s_prev.astype(BF16)
            ys, sn = [], []
            for hh in range(2):
                h = 2 * p + hh
                lcol = jnp.broadcast_to(lc[:, h:h + 1], (CQ, CQ))
                lmat = jnp.exp2(jnp.where(causal, lcol - lc_t[h:h + 1, :], -jnp.inf))
                m_h = (cb * lmat * dt_t[h:h + 1, :]).astype(BF16)
                c_s = (c_f * jnp.exp2(lcol)).astype(BF16)
                ys.append(_dot(m_h, xp) + _dot(c_s, s_prev_b))
                w_h = (b_t * w_t[h:h + 1, :]).astype(BF16)
                sn.append(_dot(w_h, xp))
            y_ref[:, lanes] = jnp.where(lo_half, ys[0], ys[1])
            st_ref[:, lanes] = s_prev * cdx[0:1, lanes] + jnp.where(lo_half, sn[0], sn[1])

    y = y_ref[...] + xa_ref[...] * dsk_ref[...]
    y = y * _silu(z_ref[...].astype(F32))
    o_ref[...] = _rms(y, nw_ref[...]).astype(BF16)


def _ssd(proj, tail, cwx, cbx, cwb, cbb, dtb, alog, dsk, nw, tri, expand, shift, batch, seq):
    nc = seq // CQ
    m = batch * seq

    def full(shape):
        return pl.BlockSpec(shape, lambda b, c: (0,) * len(shape))

    return pl.pallas_call(
        _ssd_kernel,
        grid=(batch, nc),
        in_specs=[
            pl.BlockSpec((CQ, SSD_D_INNER), lambda b, c: (b * nc + c, C_ZSSD // SSD_D_INNER)),
            pl.BlockSpec((CQ, SSD_D_INNER), lambda b, c: (b * nc + c, C_XS // SSD_D_INNER)),
            pl.BlockSpec((CQ, SSD_BC_DIM), lambda b, c: (b * nc + c, C_BC // SSD_BC_DIM)),
            pl.BlockSpec((CQ, LANES), lambda b, c: (b * nc + c, 0)),
            full((SSD_CONV_K, SSD_D_INNER)), full((1, SSD_D_INNER)),
            full((SSD_CONV_K, SSD_BC_DIM)), full((1, SSD_BC_DIM)),
            full((1, LANES)), full((1, LANES)),
            full((1, SSD_D_INNER)), full((1, SSD_D_INNER)),
            full((CQ, CQ)), full((LANES, SSD_D_INNER)), full(((SSD_CONV_K - 1) * CQ, CONV_HIST + CQ)),
        ],
        out_specs=pl.BlockSpec((CQ, SSD_D_INNER), lambda b, c: (b * nc + c, 0)),
        out_shape=jax.ShapeDtypeStruct((m, SSD_D_INNER), BF16),
        scratch_shapes=[
            pltpu.VMEM((CONV_HIST + CQ, SSD_D_INNER), BF16),
            pltpu.VMEM((CONV_HIST + CQ, SSD_BC_DIM), BF16),
            pltpu.VMEM((CQ, SSD_D_INNER), F32),
            pltpu.VMEM((CQ, SSD_BC_DIM), F32),
            pltpu.VMEM((SSD_D_STATE, SSD_D_INNER), F32),
            pltpu.VMEM((CQ, SSD_D_INNER), F32),
        ],
        compiler_params=_params(("arbitrary", "arbitrary")),
    )(proj, proj, proj, tail, cwx, cbx, cwb, cbb, dtb, alog, dsk, nw, tri, expand, shift)


N_CMP_PAD = 128
CMP_HALF = CMP_BLOCK // CMP_STRIDE


def _cmp_kernel(kv_ref, pek_ref, w1k_ref, b1k_ref, w2k_ref, pev_ref, w1v_ref, b1v_ref, w2v_ref, knw_ref,
                kc_ref, vc_ref, buf):
    hd = NSA_HEAD_DIM
    for slab in range(2 * NSA_KV_GROUPS):
        buf[slab] = kv_ref[:, slab * hd:(slab + 1) * hd].astype(F32)
    for kind, (pe_ref, w1_ref, b1_ref, w2_ref, out_ref) in enumerate((
            (pek_ref, w1k_ref, b1k_ref, w2k_ref, kc_ref), (pev_ref, w1v_ref, b1v_ref, w2v_ref, vc_ref))):
        for g in range(NSA_KV_GROUPS):
            slab = kind * NSA_KV_GROUPS + g
            u = [jnp.zeros((N_CMP_PAD, hd), F32) for _ in range(CMP_HALF)]
            for r in range(CMP_STRIDE):
                xr = buf[slab, pl.ds(r, N_CMP_PAD, stride=CMP_STRIDE), :]
                for a in range(CMP_HALF):
                    l = a * CMP_STRIDE + r
                    u[a] = u[a] + _dot((xr + pe_ref[l:l + 1, :]).astype(BF16), w1_ref[l * hd:(l + 1) * hd, :])
            acc = u[0] + pltpu.roll(u[1], N_CMP_PAD - 1, 0) + b1_ref[...]
            out = _dot(_silu(acc).astype(BF16), w2_ref[...])
            if kind == 0:
                out_ref[0, :, g * hd:(g + 1) * hd] = _rms(out, knw_ref[...]).astype(BF16)
            else:
                out_ref[0, g * hd:(g + 1) * hd, :] = out.T.astype(BF16)


def _compress(proj, pek, w1k, b1k, w2k, pev, w1v, b1v, w2v, knw, batch, seq):
    hd = NSA_HEAD_DIM

    def full(shape):
        return pl.BlockSpec(shape, lambda b: (0,) * len(shape))

    return pl.pallas_call(
        _cmp_kernel,
        grid=(batch,),
        in_specs=[
            pl.BlockSpec((seq, 2 * NSA_KV_D), lambda b: (b, C_KVCMP // (2 * NSA_KV_D))),
            full((CMP_BLOCK, hd)), full((CMP_BLOCK * hd, hd)), full((1, hd)), full((hd, hd)),
            full((CMP_BLOCK, hd)), full((CMP_BLOCK * hd, hd)), full((1, hd)), full((hd, hd)),
            full((1, hd)),
        ],
        out_specs=[pl.BlockSpec((1, N_CMP_PAD, NSA_KV_D), lambda b: (b, 0, 0)),
                   pl.BlockSpec((1, NSA_KV_D, N_CMP_PAD), lambda b: (b, 0, 0))],
        out_shape=[jax.ShapeDtypeStruct((batch, N_CMP_PAD, NSA_KV_D), BF16),
                   jax.ShapeDtypeStruct((batch, NSA_KV_D, N_CMP_PAD), BF16)],
        scratch_shapes=[pltpu.VMEM((2 * NSA_KV_GROUPS, seq, hd), F32)],
        compiler_params=_params(("arbitrary",)),
    )(proj, pek, w1k, b1k, w2k, pev, w1v, b1v, w2v, knw)


QB = Q_BLOCK
GROWS = NSA_HPG * QB
TK = 256
BR_SEL, BR_WIN = 0, 1
N_SLC = 32
N_SLC_PAD = 128
NORM_RC = 256
KT = 128
PW = 2 * QB
N_HPAIRS = NSA_HPG // 2
KX = 2 * NSA_HEAD_DIM
VT_ROWS = NSA_HEAD_DIM + 16
LOG2E = 1.4426950408889634


def _nsa_kernel(q_ref, kvs_ref, kvw_ref, kc_ref, vct_ref, z_ref, tail_ref, qnw_ref, ksnw_ref, kwnw_ref, selmap_ref,
                o_ref, kx_s, vt_s, qt_s, oc_s, m_s, acc_s, s_s, p_s):
    i = pl.program_id(1)
    qs = i * QB
    hd = NSA_HEAD_DIM
    seq = kvs_ref.shape[0]

    @pl.when(i == 0)
    def _():
        for br, src, nw in ((BR_SEL, kvs_ref, ksnw_ref), (BR_WIN, kvw_ref, kwnw_ref)):
            for g in range(NSA_KV_GROUPS):
                def body(r, carry, br=br, src=src, nw=nw, g=g):
                    r0 = pl.multiple_of(r * NORM_RC, NORM_RC)
                    kk = src[pl.ds(r0, NORM_RC), g * hd:(g + 1) * hd].astype(F32)
                    kx_s[br, pl.ds(r0, NORM_RC), g * KX:g * KX + hd] = _rms(kk, nw[...]).astype(BF16)
                    blk = (r0 + lax.broadcasted_iota(jnp.int32, (NORM_RC, hd), 0)) >> 6
                    hit = blk == lax.broadcasted_iota(jnp.int32, (NORM_RC, hd), 1)
                    aux = jnp.where(hit, 1.0 if br == BR_SEL else 0.0, 0.0)
                    kx_s[br, pl.ds(r0, NORM_RC), g * KX + hd:(g + 1) * KX] = aux.astype(BF16)
                    return carry

                lax.fori_loop(0, seq // NORM_RC, body, 0)

                def body_t(r, carry, br=br, src=src, g=g):
                    r0 = pl.multiple_of(r * KT, KT)
                    vv = src[pl.ds(r0, KT), NSA_KV_D + g * hd:NSA_KV_D + (g + 1) * hd].astype(F32)
                    vt_s[br, r, g, 0:hd, :] = vv.T.astype(BF16)
                    vt_s[br, r, g, hd:VT_ROWS, :] = jnp.ones((VT_ROWS - hd, KT), BF16)
                    return carry

                lax.fori_loop(0, seq // KT, body_t, 0)

    for h in range(NSA_HEADS):
        qh = q_ref[:, h * hd:(h + 1) * hd].astype(F32)
        qt_s[0:hd, h * QB:(h + 1) * QB] = (_rms(qh, qnw_ref[...]) * (hd ** -0.5 * LOG2E)).T.astype(BF16)

    gates_t = _sigmoid(tail_ref[...]).T

    n_sel = (qs + QB + TK - 1) // TK
    win_first = jnp.maximum(qs - (WINDOW - 1), 0) // TK
    n_win_mid = jnp.maximum(n_sel - win_first - 2, 0)
    n_plain = n_sel - 1 + n_win_mid
    n_tiles = n_plain + 2 + jnp.where(n_sel - win_first >= 2, 1, 0)

    def tile_of(u):
        e = u - n_plain
        br = jnp.where(u < n_plain, jnp.where(u < n_sel - 1, BR_SEL, BR_WIN), jnp.where(e == 0, BR_SEL, BR_WIN))
        t = jnp.where(u < n_plain, jnp.where(u < n_sel - 1, u, win_first + 1 + (u - (n_sel - 1))),
                      jnp.where(e == 1, win_first, n_sel - 1))
        return br, t

    rel0 = (lax.broadcasted_iota(jnp.int32, (TK, QB), 0) - lax.broadcasted_iota(jnp.int32, (TK, QB), 1)) - qs

    c_idx = lax.broadcasted_iota(jnp.int32, (N_CMP_PAD, GROWS), 0)
    q_in = lax.broadcasted_iota(jnp.int32, (N_CMP_PAD, GROWS), 1) & (QB - 1)
    mask_c = c_idx * CMP_STRIDE + (CMP_BLOCK - 1) <= qs + q_in

    jj = lax.broadcasted_iota(jnp.int32, (N_SLC, QB), 0)
    tq_t = qs + lax.broadcasted_iota(jnp.int32, (N_SLC, QB), 1)

    for g in range(NSA_KV_GROUPS):
        gl = slice(g * hd, (g + 1) * hd)
        gcols = slice(g * GROWS, (g + 1) * GROWS)
        s = _dot(kc_ref[0, :, gl], qt_s[0:hd, gcols])
        s = jnp.where(mask_c, s, NEG_INF)
        e = jnp.exp2(s - jnp.max(s, axis=0, keepdims=True))
        p_cmp = jnp.where(mask_c, e / jnp.sum(e, axis=0, keepdims=True), 0.0)
        oc_s[:, gcols] = _dot(vct_ref[0, gl, :], p_cmp.astype(BF16))

        p_sum = p_cmp[:, 0:QB]
        for k in range(1, NSA_HPG):
            p_sum = p_sum + p_cmp[:, k * QB:(k + 1) * QB]
        imp = _dot(selmap_ref[...], p_sum, precision=HIGHEST)[0:N_SLC, :]
        forced = (jj == (tq_t >> 6)) | (jj == 0)
        imp = jnp.where(forced, FORCED_SCORE, jnp.where(jj * SEL_BLOCK <= tq_t, imp, -1.0))
        rank = jnp.zeros((N_SLC, QB), jnp.int32)
        for j2 in range(N_SLC):
            other = imp[j2:j2 + 1, :]
            beats = (other > imp) | ((other == imp) & (jj > j2))
            rank = rank + jnp.where(beats, 1, 0)
        picked = (rank < SEL_TOPK) & (jj * SEL_BLOCK <= tq_t)
        sel_bias = jnp.concatenate([jnp.where(picked, 0.0, NEG_INF).astype(F32),
                                    jnp.zeros((hd - N_SLC, QB), F32)], axis=0).astype(BF16)
        for k in range(NSA_HPG):
            qt_s[hd:KX, (g * NSA_HPG + k) * QB:(g * NSA_HPG + k + 1) * QB] = sel_bias

    m_s[...] = jnp.full_like(m_s, NEG_INF)
    acc_s[...] = jnp.zeros_like(acc_s)
    units = [(g, slice((g * N_HPAIRS + pr) * PW, (g * N_HPAIRS + pr + 1) * PW))
             for g in range(NSA_KV_GROUPS) for pr in range(N_HPAIRS)]

    def keys_of(br, t):
        k0 = pl.multiple_of(t * TK, TK)
        return [kx_s[br, pl.ds(k0, TK), g * KX:(g + 1) * KX] for g in range(NSA_KV_GROUPS)]

    def values_of(br, t):
        return [jnp.concatenate([vt_s[br, t * (TK // KT) + u, g] for u in range(TK // KT)], axis=1)
                for g in range(NSA_KV_GROUPS)]

    def finish_unit(br, v_tt, g, cols, alpha):
        acc_s[br, :, cols] = alpha * acc_s[br, :, cols] + _dot(v_tt[g], p_s[:, cols])

    def finish(br, t, alphas):
        v_tt = values_of(br, t)
        for (g, cols), alpha in zip(units, alphas):
            finish_unit(br, v_tt, g, cols, alpha)

    def make_body(edge):
        def body(u, carry):
            alphas_prev, br_prev, t_prev = carry
            br, t = tile_of(u)
            k_next = keys_of(*tile_of(jnp.minimum(u + 1, n_tiles - 1)))
            v_prev = values_of(br_prev, t_prev)
            if edge:
                rel = rel0 + t * TK
                bias = jnp.where((rel <= 0) & (rel > jnp.where(br == BR_WIN, -WINDOW, -(1 << 30))), 0.0, NEG_INF)
                bias = jnp.concatenate([bias, bias], axis=1)
            alphas = []
            for (g, cols), alpha_prev in zip(units, alphas_prev):
                s2 = s_s[:, cols]
                if edge:
                    s2 = s2 + bias
                s_s[:, cols] = _dot(k_next[g], qt_s[:, cols])
                finish_unit(br_prev, v_prev, g, cols, alpha_prev)
                m_prev = m_s[br, :, cols]
                m_new = jnp.maximum(m_prev, jnp.max(s2, axis=0, keepdims=True))
                m_s[br, :, cols] = m_new
                p_s[:, cols] = jnp.exp2(s2 - m_new).astype(BF16)
                alphas.append(jnp.exp2(m_prev - m_new))
            return tuple(alphas), br, t
        return body

    k_first = keys_of(*tile_of(0))
    for g, cols in units:
        s_s[:, cols] = _dot(k_first[g], qt_s[:, cols])
    p_s[...] = jnp.zeros_like(p_s)
    carry = (tuple(jnp.ones((1, PW), F32) for _ in units), jnp.int32(BR_SEL), jnp.int32(0))
    carry = lax.fori_loop(0, n_plain, make_body(False), carry)
    alphas_last, br_last, t_last = lax.fori_loop(n_plain, n_tiles, make_body(True), carry)
    finish(br_last, t_last, alphas_last)

    for g, cols in units:
        heads = [cols.start // QB + u for u in range(2)]

        def gate_row(branch):
            return jnp.concatenate([gates_t[T_GATE + h * N_NSA_BRANCHES + branch:
                                            T_GATE + h * N_NSA_BRANCHES + branch + 1, :] for h in heads], axis=1)

        o_t = (gate_row(0) * oc_s[:, cols]
               + (gate_row(1) / acc_s[BR_SEL, hd:hd + 1, cols]) * acc_s[BR_SEL, 0:hd, cols]
               + (gate_row(2) / acc_s[BR_WIN, hd:hd + 1, cols]) * acc_s[BR_WIN, 0:hd, cols])
        for u, h in enumerate(heads):
            zz = z_ref[:, h * hd:(h + 1) * hd].astype(F32)
            o_ref[:, h * hd:(h + 1) * hd] = (o_t[:, u * QB:(u + 1) * QB].T * _silu(zz)).astype(BF16)


def _nsa(proj, tail, kc, vc, qnw, ksnw, kwnw, selmap_t, batch, seq):
    nq = seq // QB
    hd = NSA_HEAD_DIM
    m = batch * seq

    def full(shape):
        return pl.BlockSpec(shape, lambda b, i: (0,) * len(shape))

    return pl.pallas_call(
        _nsa_kernel,
        grid=(batch, nq),
        in_specs=[
            pl.BlockSpec((QB, NSA_D), lambda b, i: (b * nq + i, C_Q // NSA_D)),
            pl.BlockSpec((seq, 2 * NSA_KV_D), lambda b, i: (b, C_KVSLC // (2 * NSA_KV_D))),
            pl.BlockSpec((seq, 2 * NSA_KV_D), lambda b, i: (b, C_KVWIN // (2 * NSA_KV_D))),
            pl.BlockSpec((1, N_CMP_PAD, NSA_KV_D), lambda b, i: (b, 0, 0)),
            pl.BlockSpec((1, NSA_KV_D, N_CMP_PAD), lambda b, i: (b, 0, 0)),
            pl.BlockSpec((QB, NSA_D), lambda b, i: (b * nq + i, C_ZNSA // NSA_D)),
            pl.BlockSpec((QB, LANES), lambda b, i: (b * nq + i, 0)),
            full((1, hd)), full((1, hd)), full((1, hd)),
            full((N_SLC_PAD, N_CMP_PAD)),
        ],
        out_specs=pl.BlockSpec((QB, NSA_D), lambda b, i: (b * nq + i, 0)),
        out_shape=jax.ShapeDtypeStruct((m, NSA_D), BF16),
        scratch_shapes=[
            pltpu.VMEM((2, seq, NSA_KV_GROUPS * KX), BF16),
            pltpu.VMEM((2, seq // KT, NSA_KV_GROUPS, VT_ROWS, KT), BF16),
            pltpu.VMEM((KX, NSA_HEADS * QB), BF16),
            pltpu.VMEM((hd, NSA_HEADS * QB), F32),
            pltpu.VMEM((2, 1, NSA_HEADS * QB), F32),
            pltpu.VMEM((2, VT_ROWS, NSA_HEADS * QB), F32),
            pltpu.VMEM((TK, NSA_HEADS * QB), F32),
            pltpu.VMEM((TK, NSA_HEADS * QB), BF16),
        ],
        compiler_params=_params(("arbitrary", "arbitrary")),
    )(proj, proj, proj, kc, vc, proj, tail, qnw, ksnw, kwnw, selmap_t)


OUT_TM = 1024
OUT_TN = 512


def _merge_kernel(ys_ref, yn_ref, gs_ref, gn_ref, ws_ref, wn_ref, o_ref):
    a = _dot(ys_ref[...], ws_ref[...])
    b = _dot(yn_ref[...], wn_ref[...])
    merged = _sigmoid(gs_ref[...].astype(F32)) * a + _sigmoid(gn_ref[...].astype(F32)) * b
    o_ref[...] = merged.astype(BF16)


def _merge(y_ssd, y_nsa, proj, w_s, w_n):
    m = y_ssd.shape[0]
    nb = D_MODEL // OUT_TN
    return pl.pallas_call(
        _merge_kernel,
        grid=(nb, m // OUT_TM),
        in_specs=[
            pl.BlockSpec((OUT_TM, D_MODEL), lambda j, i: (i, 0)),
            pl.BlockSpec((OUT_TM, D_MODEL), lambda j, i: (i, 0)),
            pl.BlockSpec((OUT_TM, OUT_TN), lambda j, i: (i, C_GLSSD // OUT_TN + j)),
            pl.BlockSpec((OUT_TM, OUT_TN), lambda j, i: (i, C_GLNSA // OUT_TN + j)),
            pl.BlockSpec((D_MODEL, OUT_TN), lambda j, i: (0, j)),
            pl.BlockSpec((D_MODEL, OUT_TN), lambda j, i: (0, j)),
        ],
        out_specs=pl.BlockSpec((OUT_TM, OUT_TN), lambda j, i: (i, j)),
        out_shape=jax.ShapeDtypeStruct((m, D_MODEL), BF16),
        compiler_params=_params(("arbitrary", "arbitrary")),
    )(y_ssd, y_nsa, proj, proj, w_s, w_n)


def _final_kernel(m_ref, x_ref, w_ref, o_ref):
    o_ref[...] = x_ref[...] + _dot(m_ref[...], w_ref[...])


def _final(merged, x2, w_o):
    m = merged.shape[0]
    nb = D_MODEL // OUT_TN
    return pl.pallas_call(
        _final_kernel,
        grid=(nb, m // OUT_TM),
        in_specs=[
            pl.BlockSpec((OUT_TM, D_MODEL), lambda j, i: (i, 0)),
            pl.BlockSpec((OUT_TM, OUT_TN), lambda j, i: (i, j)),
            pl.BlockSpec((D_MODEL, OUT_TN), lambda j, i: (0, j)),
        ],
        out_specs=pl.BlockSpec((OUT_TM, OUT_TN), lambda j, i: (i, j)),
        out_shape=jax.ShapeDtypeStruct((m, D_MODEL), F32),
        compiler_params=_params(("arbitrary", "arbitrary")),
    )(merged, x2, w_o)


def _w_in_sections(w_in):
    sec = {name: w_in[:, _OFFS[i]:_OFFS[i + 1]] for i, name in enumerate(
        ("z_ssd", "xbc", "dt", "q", "k_cmp", "v_cmp", "k_slc", "v_slc", "k_win", "v_win",
         "z_nsa", "gates", "gl_ssd", "gl_nsa"))}
    main = jnp.concatenate([
        sec["z_ssd"], sec["xbc"][:, :SSD_D_INNER], sec["q"], sec["z_nsa"], sec["gl_ssd"], sec["gl_nsa"],
        sec["xbc"][:, SSD_D_INNER:], sec["k_cmp"], sec["v_cmp"], sec["k_slc"], sec["v_slc"],
        sec["k_win"], sec["v_win"]], axis=1).astype(BF16)
    pad = LANES - SSD_HEADS - NSA_HEADS * N_NSA_BRANCHES
    tail = jnp.concatenate([sec["dt"], sec["gates"], jnp.zeros((D_MODEL, pad), w_in.dtype)], axis=1).astype(BF16)
    return main, tail


def _constants():
    tri = np.tril(np.ones((CQ, CQ), np.float32))
    expand = np.zeros((LANES, SSD_D_INNER), np.float32)
    for h in range(SSD_HEADS):
        expand[h, h * SSD_HEAD_DIM:(h + 1) * SSD_HEAD_DIM] = 1.0
    n_cmp = N_CMP_PAD - 1
    ci = np.arange(n_cmp)[:, None] * CMP_STRIDE
    sj = np.arange(32)[None, :] * SEL_BLOCK
    sel_map = ((ci < sj + SEL_BLOCK) & (ci + CMP_BLOCK > sj)).astype(np.float32)
    selmap_t = np.zeros((N_SLC_PAD, N_CMP_PAD), np.float32)
    selmap_t[:32, :n_cmp] = sel_map.T
    shift = np.zeros(((SSD_CONV_K - 1) * CQ, CONV_HIST + CQ), np.float32)
    for k in range(SSD_CONV_K - 1):
        for t in range(CQ):
            shift[k * CQ + t, CONV_HIST + t - (SSD_CONV_K - 1) + k] = 1.0
    return jnp.asarray(tri), jnp.asarray(expand), jnp.asarray(selmap_t), jnp.asarray(shift, BF16)


def _pad_lanes(v):
    return jnp.concatenate([v.astype(F32), jnp.zeros((LANES - v.shape[0],), F32)])[None, :]


def kernel(x, norm_w, w_in, conv_w, conv_b, dt_bias, a_log, d_skip, ssd_norm_w, q_norm_w, k_cmp_norm_w,
           k_slc_norm_w, k_win_norm_w, cmp_pe_k, cmp_w1_k, cmp_b1_k, cmp_w2_k, cmp_pe_v, cmp_w1_v, cmp_b1_v,
           cmp_w2_v, w_out_ssd, w_out_nsa, w_o):
    batch, seq, _ = x.shape
    assert seq == 2048 and (batch * seq) % IN_TM == 0
    x2 = x.reshape(batch * seq, D_MODEL)
    w_main, w_tail = _w_in_sections(w_in)
    tri, expand, selmap_t, shift = _constants()

    proj, tail = _inproj(x2, norm_w[None, :], w_main, w_tail)

    y_ssd = _ssd(proj, tail,
                 conv_w[:, :SSD_D_INNER], conv_b[None, :SSD_D_INNER],
                 conv_w[:, SSD_D_INNER:], conv_b[None, SSD_D_INNER:],
                 _pad_lanes(dt_bias), _pad_lanes(a_log),
                 jnp.repeat(d_skip, SSD_HEAD_DIM)[None, :], ssd_norm_w[None, :],
                 tri, expand, shift, batch, seq)

    kc, vc = _compress(proj, cmp_pe_k, cmp_w1_k.astype(BF16), cmp_b1_k[None, :], cmp_w2_k.astype(BF16),
                       cmp_pe_v, cmp_w1_v.astype(BF16), cmp_b1_v[None, :], cmp_w2_v.astype(BF16),
                       k_cmp_norm_w[None, :], batch, seq)

    y_nsa = _nsa(proj, tail, kc, vc, q_norm_w[None, :], k_slc_norm_w[None, :], k_win_norm_w[None, :],
                 selmap_t, batch, seq)

    merged = _merge(y_ssd, y_nsa, proj, w_out_ssd.astype(BF16), w_out_nsa.astype(BF16))
    out = _final(merged, x2, w_o.astype(BF16))
    return out.reshape(batch, seq, D_MODEL)
```

```python
import functools

import numpy as np
import jax
import jax.numpy as jnp
from jax import lax
from jax.experimental import pallas as pl
from jax.experimental.pallas import tpu as pltpu

D_MODEL = 2048
SSD_D_INNER = D_MODEL
SSD_HEAD_DIM = 64
SSD_HEADS = SSD_D_INNER // SSD_HEAD_DIM
SSD_GROUPS = 4
SSD_D_STATE = 128
SSD_CONV_K = 4
SSD_CHUNK = 128
SSD_BC_DIM = 2 * SSD_GROUPS * SSD_D_STATE
SSD_CONV_DIM = SSD_D_INNER + SSD_BC_DIM

NSA_HEADS = 16
NSA_HEAD_DIM = 128
NSA_KV_GROUPS = 2
NSA_HPG = NSA_HEADS // NSA_KV_GROUPS
NSA_D = NSA_HEADS * NSA_HEAD_DIM
NSA_KV_D = NSA_KV_GROUPS * NSA_HEAD_DIM
N_NSA_BRANCHES = 3
CMP_BLOCK = 32
CMP_STRIDE = 16
SEL_BLOCK = 64
SEL_TOPK = 8
WINDOW = 512
Q_BLOCK = 128

EPS = 1e-6
NEG_INF = -1e30
FORCED_SCORE = 1e9

LANES = 128
SUBLANES = 8
VMEM_LIMIT = 56 * 1024 * 1024

F32 = jnp.float32
BF16 = jnp.bfloat16
HIGHEST = lax.Precision.HIGHEST

C_ZSSD = 0
C_XS = 2048
C_Q = 4096
C_ZNSA = 6144
C_GLSSD = 8192
C_GLNSA = 10240
C_BC = 12288
C_KVCMP = 13312
C_KVSLC = 13824
C_KVWIN = 14336
N_MAIN = 14848
T_GATE = SSD_HEADS

_SIZES = [SSD_D_INNER, SSD_CONV_DIM, SSD_HEADS, NSA_D] + [NSA_KV_D] * 6 + [NSA_D, NSA_HEADS * N_NSA_BRANCHES, D_MODEL, D_MODEL]
_OFFS = [0] + [int(o) for o in np.cumsum(_SIZES)]


def _sigmoid(v):
    return 0.5 + 0.5 * jnp.tanh(0.5 * v)


def _silu(v):
    h = 0.5 * v
    return h + h * jnp.tanh(h)


def _softplus(v):
    return jnp.maximum(v, 0.0) + jnp.log(1.0 + jnp.exp(-jnp.abs(v)))


def _rms(v, w):
    ms = jnp.mean(v * v, axis=-1, keepdims=True)
    return v * lax.rsqrt(ms + EPS) * w


def _dot(a, b, precision=None):
    return jnp.dot(a, b, preferred_element_type=F32, precision=precision)


def _split3(v):
    hi = v.astype(BF16)
    r1 = v - hi.astype(F32)
    mid = r1.astype(BF16)
    lo = (r1 - mid.astype(F32)).astype(BF16)
    return hi, mid, lo


def _dot_nt(a, b, precision=None):
    return lax.dot_general(a, b, (((1,), (1,)), ((), ())), preferred_element_type=F32, precision=precision)


def _params(sem):
    return pltpu.CompilerParams(dimension_semantics=sem, vmem_limit_bytes=VMEM_LIMIT)


IN_TM = 1024
IN_TN = 512
IN_RC = 256


def _inproj_kernel(x_ref, nw_ref, w_ref, wt_ref, o_ref, t_ref, h_ref):
    j = pl.program_id(1)

    @pl.when(j == 0)
    def _():
        def body(r, carry):
            r0 = pl.multiple_of(r * IN_RC, IN_RC)
            h_ref[pl.ds(r0, IN_RC), :] = _rms(x_ref[pl.ds(r0, IN_RC), :], nw_ref[...]).astype(BF16)
            return carry

        lax.fori_loop(0, IN_TM // IN_RC, body, 0)
        t_ref[...] = _dot(h_ref[...], wt_ref[...])

    o_ref[...] = _dot(h_ref[...], w_ref[...]).astype(BF16)


def _inproj(x2, norm_w, w_main, w_tail):
    m = x2.shape[0]
    return pl.pallas_call(
        _inproj_kernel,
        grid=(m // IN_TM, N_MAIN // IN_TN),
        in_specs=[
            pl.BlockSpec((IN_TM, D_MODEL), lambda i, j: (i, 0)),
            pl.BlockSpec((1, D_MODEL), lambda i, j: (0, 0)),
            pl.BlockSpec((D_MODEL, IN_TN), lambda i, j: (0, j)),
            pl.BlockSpec((D_MODEL, LANES), lambda i, j: (0, 0)),
        ],
        out_specs=[
            pl.BlockSpec((IN_TM, IN_TN), lambda i, j: (i, j)),
            pl.BlockSpec((IN_TM, LANES), lambda i, j: (i, 0)),
        ],
        out_shape=[
            jax.ShapeDtypeStruct((m, N_MAIN), BF16),
            jax.ShapeDtypeStruct((m, LANES), F32),
        ],
        scratch_shapes=[pltpu.VMEM((IN_TM, D_MODEL), BF16)],
        compiler_params=_params(("arbitrary", "arbitrary")),
    )(x2, norm_w, w_main, w_tail)


CQ = SSD_CHUNK
PAIR = 2 * SSD_HEAD_DIM
N_PAIRS = SSD_HEADS // 2
PAIRS_PER_GROUP = N_PAIRS // SSD_GROUPS
CONV_HIST = 16
CONV_COLS = 256


def _ssd_kernel(z_ref, xs_ref, bc_ref, tail_ref, cwx_ref, cbx_ref, cwb_ref, cbb_ref, dtb_ref, alog_ref,
                dsk_ref, nw_ref, tri_ref, exp_ref, shift_ref, o_ref, xbuf, bbuf, xa_ref, ba_ref, st_ref, y_ref):
    c = pl.program_id(1)

    @pl.when(c == 0)
    def _():
        xbuf[0:CONV_HIST, :] = jnp.zeros((CONV_HIST, SSD_D_INNER), BF16)
        bbuf[0:CONV_HIST, :] = jnp.zeros((CONV_HIST, SSD_BC_DIM), BF16)
        st_ref[...] = jnp.zeros_like(st_ref)

    xbuf[CONV_HIST:CONV_HIST + CQ, :] = xs_ref[...]
    bbuf[CONV_HIST:CONV_HIST + CQ, :] = bc_ref[...]

    def conv(buf, cur_ref, w_ref, b_ref, out_ref):
        for j in range(buf.shape[1] // CONV_COLS):
            cs = slice(j * CONV_COLS, (j + 1) * CONV_COLS)
            shifted = _dot(shift_ref[...], buf[:, cs])
            acc = b_ref[:, cs] + cur_ref[:, cs].astype(F32) * w_ref[SSD_CONV_K - 1:SSD_CONV_K, cs]
            for k in range(SSD_CONV_K - 1):
                acc = acc + shifted[k * CQ:(k + 1) * CQ, :] * w_ref[k:k + 1, cs]
            out_ref[:, cs] = _silu(acc)

    conv(xbuf, xs_ref, cwx_ref, cbx_ref, xa_ref)
    conv(bbuf, bc_ref, cwb_ref, cbb_ref, ba_ref)
    xbuf[0:CONV_HIST, :] = xbuf[CQ:CQ + CONV_HIST, :]
    bbuf[0:CONV_HIST, :] = bbuf[CQ:CQ + CONV_HIST, :]

    dt = _softplus(tail_ref[...] + dtb_ref[...])
    la = dt * (-jnp.exp(alog_ref[...]))
    lc = sum(_dot(tri_ref[...], part) for part in _split3(la)) * LOG2E
    lc_t = lc.T
    dt_t = dt.T
    w_t = jnp.exp2(lc_t[:, CQ - 1:CQ] - lc_t) * dt_t
    cd = jnp.exp2(lc[CQ - 1:CQ, :])
    cdx = sum(_dot(part, exp_ref[...]) for part in _split3(jnp.broadcast_to(cd, (SUBLANES, LANES))))

    row = lax.broadcasted_iota(jnp.int32, (CQ, CQ), 0)
    col = lax.broadcasted_iota(jnp.int32, (CQ, CQ), 1)
    causal = row >= col
    lo_half = col < SSD_HEAD_DIM

    for g in range(SSD_GROUPS):
        b_f = ba_ref[:, g * SSD_D_STATE:(g + 1) * SSD_D_STATE]
        c_f = ba_ref[:, (SSD_GROUPS + g) * SSD_D_STATE:(SSD_GROUPS + g + 1) * SSD_D_STATE]
        cb = _dot_nt(c_f.astype(BF16), b_f.astype(BF16))
        b_t = b_f.T
        for pp in range(PAIRS_PER_GROUP):
            p = g * PAIRS_PER_GROUP + pp
            lanes = slice(p * PAIR, (p + 1) * PAIR)
            xp = xa_ref[:, lanes].astype(BF16)
            s_prev = st_ref[:, lanes]
            s_prev_b = s_prev.astype(BF16)
            x_half = (jnp.where(lo_half, xp, 0.0), jnp.where(lo_half, 0.0, xp))
            s_half = (jnp.where(lo_half, s_prev_b, 0.0), jnp.where(lo_half, 0.0, s_prev_b))
            y_pair, s_new = None, None
            for hh in range(2):
                h = 2 * p + hh
                lcol = jnp.broadcast_to(lc[:, h:h + 1], (CQ, CQ))
                lmat = jnp.exp2(jnp.where(causal, lcol - lc_t[h:h + 1, :], -jnp.inf))
                m_h = (cb * lmat * dt_t[h:h + 1, :]).astype(BF16)
                c_s = (c_f * jnp.exp2(lcol)).astype(BF16)
                y_h = _dot(m_h, x_half[hh]) + _dot(c_s, s_half[hh])
                w_h = (b_t * w_t[h:h + 1, :]).astype(BF16)
                s_h = _dot(w_h, x_half[hh])
                y_pair = y_h if y_pair is None else y_pair + y_h
                s_new = s_h if s_new is None else s_new + s_h
            y_ref[:, lanes] = y_pair
            st_ref[:, lanes] = s_prev * cdx[0:1, lanes] + s_new

    y = y_ref[...] + xa_ref[...] * dsk_ref[...]
    y = y * _silu(z_ref[...].astype(F32))
    o_ref[...] = _rms(y, nw_ref[...]).astype(BF16)


def _ssd(proj, tail, cwx, cbx, cwb, cbb, dtb, alog, dsk, nw, tri, expand, shift, batch, seq):
    nc = seq // CQ
    m = batch * seq

    def full(shape):
        return pl.BlockSpec(shape, lambda b, c: (0,) * len(shape))

    return pl.pallas_call(
        _ssd_kernel,
        grid=(batch, nc),
        in_specs=[
            pl.BlockSpec((CQ, SSD_D_INNER), lambda b, c: (b * nc + c, C_ZSSD // SSD_D_INNER)),
            pl.BlockSpec((CQ, SSD_D_INNER), lambda b, c: (b * nc + c, C_XS // SSD_D_INNER)),
            pl.BlockSpec((CQ, SSD_BC_DIM), lambda b, c: (b * nc + c, C_BC // SSD_BC_DIM)),
            pl.BlockSpec((CQ, LANES), lambda b, c: (b * nc + c, 0)),
            full((SSD_CONV_K, SSD_D_INNER)), full((1, SSD_D_INNER)),
            full((SSD_CONV_K, SSD_BC_DIM)), full((1, SSD_BC_DIM)),
            full((1, LANES)), full((1, LANES)),
            full((1, SSD_D_INNER)), full((1, SSD_D_INNER)),
            full((CQ, CQ)), full((LANES, SSD_D_INNER)), full(((SSD_CONV_K - 1) * CQ, CONV_HIST + CQ)),
        ],
        out_specs=pl.BlockSpec((CQ, SSD_D_INNER), lambda b, c: (b * nc + c, 0)),
        out_shape=jax.ShapeDtypeStruct((m, SSD_D_INNER), BF16),
        scratch_shapes=[
            pltpu.VMEM((CONV_HIST + CQ, SSD_D_INNER), BF16),
            pltpu.VMEM((CONV_HIST + CQ, SSD_BC_DIM), BF16),
            pltpu.VMEM((CQ, SSD_D_INNER), F32),
            pltpu.VMEM((CQ, SSD_BC_DIM), F32),
            pltpu.VMEM((SSD_D_STATE, SSD_D_INNER), F32),
            pltpu.VMEM((CQ, SSD_D_INNER), F32),
        ],
        compiler_params=_params(("arbitrary", "arbitrary")),
    )(proj, proj, proj, tail, cwx, cbx, cwb, cbb, dtb, alog, dsk, nw, tri, expand, shift)


N_CMP_PAD = 128
CMP_HALF = CMP_BLOCK // CMP_STRIDE


def _cmp_kernel(kv_ref, pek_ref, w1k_ref, b1k_ref, w2k_ref, pev_ref, w1v_ref, b1v_ref, w2v_ref, knw_ref,
                kc_ref, vc_ref, buf):
    hd = NSA_HEAD_DIM
    for slab in range(2 * NSA_KV_GROUPS):
        buf[slab] = kv_ref[:, slab * hd:(slab + 1) * hd].astype(F32)
    for kind, (pe_ref, w1_ref, b1_ref, w2_ref, out_ref) in enumerate((
            (pek_ref, w1k_ref, b1k_ref, w2k_ref, kc_ref), (pev_ref, w1v_ref, b1v_ref, w2v_ref, vc_ref))):
        for g in range(NSA_KV_GROUPS):
            slab = kind * NSA_KV_GROUPS + g
            u = [jnp.zeros((N_CMP_PAD, hd), F32) for _ in range(CMP_HALF)]
            for r in range(CMP_STRIDE):
                xr = buf[slab, pl.ds(r, N_CMP_PAD, stride=CMP_STRIDE), :]
                for a in range(CMP_HALF):
                    l = a * CMP_STRIDE + r
                    u[a] = u[a] + _dot((xr + pe_ref[l:l + 1, :]).astype(BF16), w1_ref[l * hd:(l + 1) * hd, :])
            acc = u[0] + pltpu.roll(u[1], N_CMP_PAD - 1, 0) + b1_ref[...]
            out = _dot(_silu(acc).astype(BF16), w2_ref[...])
            if kind == 0:
                out_ref[0, :, g * hd:(g + 1) * hd] = _rms(out, knw_ref[...]).astype(BF16)
            else:
                out_ref[0, g * hd:(g + 1) * hd, :] = out.T.astype(BF16)


def _compress(proj, pek, w1k, b1k, w2k, pev, w1v, b1v, w2v, knw, batch, seq):
    hd = NSA_HEAD_DIM

    def full(shape):
        return pl.BlockSpec(shape, lambda b: (0,) * len(shape))

    return pl.pallas_call(
        _cmp_kernel,
        grid=(batch,),
        in_specs=[
            pl.BlockSpec((seq, 2 * NSA_KV_D), lambda b: (b, C_KVCMP // (2 * NSA_KV_D))),
            full((CMP_BLOCK, hd)), full((CMP_BLOCK * hd, hd)), full((1, hd)), full((hd, hd)),
            full((CMP_BLOCK, hd)), full((CMP_BLOCK * hd, hd)), full((1, hd)), full((hd, hd)),
            full((1, hd)),
        ],
        out_specs=[pl.BlockSpec((1, N_CMP_PAD, NSA_KV_D), lambda b: (b, 0, 0)),
                   pl.BlockSpec((1, NSA_KV_D, N_CMP_PAD), lambda b: (b, 0, 0))],
        out_shape=[jax.ShapeDtypeStruct((batch, N_CMP_PAD, NSA_KV_D), BF16),
                   jax.ShapeDtypeStruct((batch, NSA_KV_D, N_CMP_PAD), BF16)],
        scratch_shapes=[pltpu.VMEM((2 * NSA_KV_GROUPS, seq, hd), F32)],
        compiler_params=_params(("arbitrary",)),
    )(proj, pek, w1k, b1k, w2k, pev, w1v, b1v, w2v, knw)


QB = Q_BLOCK
GROWS = NSA_HPG * QB
TK = 256
BR_SEL, BR_WIN = 0, 1
N_SLC = 32
N_SLC_PAD = 128
NORM_RC = 256
KT = 128
PW = 2 * QB
N_HPAIRS = NSA_HPG // 2
KX = 2 * NSA_HEAD_DIM
VT_ROWS = NSA_HEAD_DIM + 16
LOG2E = 1.4426950408889634


def _nsa_kernel(q_ref, kvs_ref, kvw_ref, kc_ref, vct_ref, z_ref, tail_ref, qnw_ref, ksnw_ref, kwnw_ref, selmap_ref,
                o_ref, kx_s, vt_s, qt_s, oc_s, m_s, acc_s, s_s, p_s):
    i = pl.program_id(1)
    qs = i * QB
    hd = NSA_HEAD_DIM
    seq = kvs_ref.shape[0]

    @pl.when(i == 0)
    def _():
        for br, src, nw in ((BR_SEL, kvs_ref, ksnw_ref), (BR_WIN, kvw_ref, kwnw_ref)):
            for g in range(NSA_KV_GROUPS):
                def body(r, carry, br=br, src=src, nw=nw, g=g):
                    r0 = pl.multiple_of(r * NORM_RC, NORM_RC)
                    kk = src[pl.ds(r0, NORM_RC), g * hd:(g + 1) * hd].astype(F32)
                    kx_s[br, pl.ds(r0, NORM_RC), g * KX:g * KX + hd] = _rms(kk, nw[...]).astype(BF16)
                    blk = (r0 + lax.broadcasted_iota(jnp.int32, (NORM_RC, hd), 0)) >> 6
                    hit = blk == lax.broadcasted_iota(jnp.int32, (NORM_RC, hd), 1)
                    aux = jnp.where(hit, 1.0 if br == BR_SEL else 0.0, 0.0)
                    kx_s[br, pl.ds(r0, NORM_RC), g * KX + hd:(g + 1) * KX] = aux.astype(BF16)
                    return carry

                lax.fori_loop(0, seq // NORM_RC, body, 0)

                def body_t(r, carry, br=br, src=src, g=g):
                    r0 = pl.multiple_of(r * KT, KT)
                    vv = src[pl.ds(r0, KT), NSA_KV_D + g * hd:NSA_KV_D + (g + 1) * hd].astype(F32)
                    vt_s[br, r, g, 0:hd, :] = vv.T.astype(BF16)
                    vt_s[br, r, g, hd:VT_ROWS, :] = jnp.ones((VT_ROWS - hd, KT), BF16)
                    return carry

                lax.fori_loop(0, seq // KT, body_t, 0)

    for h in range(NSA_HEADS):
        qh = q_ref[:, h * hd:(h + 1) * hd].astype(F32)
        qt_s[0:hd, h * QB:(h + 1) * QB] = (_rms(qh, qnw_ref[...]) * (hd ** -0.5 * LOG2E)).T.astype(BF16)

    gates_t = _sigmoid(tail_ref[...]).T

    n_sel = (qs + QB + TK - 1) // TK
    win_first = jnp.maximum(qs - (WINDOW - 1), 0) // TK
    n_win_mid = jnp.maximum(n_sel - win_first - 2, 0)
    n_plain = n_sel - 1 + n_win_mid
    n_tiles = n_plain + 2 + jnp.where(n_sel - win_first >= 2, 1, 0)

    def tile_of(u):
        e = u - n_plain
        br = jnp.where(u < n_plain, jnp.where(u < n_sel - 1, BR_SEL, BR_WIN), jnp.where(e == 0, BR_SEL, BR_WIN))
        t = jnp.where(u < n_plain, jnp.where(u < n_sel - 1, u, win_first + 1 + (u - (n_sel - 1))),
                      jnp.where(e == 1, win_first, n_sel - 1))
        return br, t

    rel0 = (lax.broadcasted_iota(jnp.int32, (TK, QB), 0) - lax.broadcasted_iota(jnp.int32, (TK, QB), 1)) - qs

    c_idx = lax.broadcasted_iota(jnp.int32, (N_CMP_PAD, GROWS), 0)
    q_in = lax.broadcasted_iota(jnp.int32, (N_CMP_PAD, GROWS), 1) & (QB - 1)
    mask_c = c_idx * CMP_STRIDE + (CMP_BLOCK - 1) <= qs + q_in

    jj = lax.broadcasted_iota(jnp.int32, (N_SLC, QB), 0)
    tq_t = qs + lax.broadcasted_iota(jnp.int32, (N_SLC, QB), 1)

    for g in range(NSA_KV_GROUPS):
        gl = slice(g * hd, (g + 1) * hd)
        gcols = slice(g * GROWS, (g + 1) * GROWS)
        s = _dot(kc_ref[0, :, gl], qt_s[0:hd, gcols])
        s = jnp.where(mask_c, s, NEG_INF)
        e = jnp.exp2(s - jnp.max(s, axis=0, keepdims=True))
        p_cmp = jnp.where(mask_c, e / jnp.sum(e, axis=0, keepdims=True), 0.0)
        oc_s[:, gcols] = _dot(vct_ref[0, gl, :], p_cmp.astype(BF16))

        p_sum = p_cmp[:, 0:QB]
        for k in range(1, NSA_HPG):
            p_sum = p_sum + p_cmp[:, k * QB:(k + 1) * QB]
        imp = sum(_dot(selmap_ref[...], part) for part in _split3(p_sum))[0:N_SLC, :]
        forced = (jj == (tq_t >> 6)) | (jj == 0)
        imp = jnp.where(forced, FORCED_SCORE, jnp.where(jj * SEL_BLOCK <= tq_t, imp, -1.0))
        rank = jnp.zeros((N_SLC, QB), jnp.int32)
        for j2 in range(N_SLC):
            other = imp[j2:j2 + 1, :]
            beats = (other > imp) | ((other == imp) & (jj > j2))
            rank = rank + jnp.where(beats, 1, 0)
        picked = (rank < SEL_TOPK) & (jj * SEL_BLOCK <= tq_t)
        sel_bias = jnp.concatenate([jnp.where(picked, 0.0, NEG_INF).astype(F32),
                                    jnp.zeros((hd - N_SLC, QB), F32)], axis=0).astype(BF16)
        for k in range(NSA_HPG):
            qt_s[hd:KX, (g * NSA_HPG + k) * QB:(g * NSA_HPG + k + 1) * QB] = sel_bias

    m_s[...] = jnp.full_like(m_s, NEG_INF)
    acc_s[...] = jnp.zeros_like(acc_s)
    units = [(g, slice((g * N_HPAIRS + pr) * PW, (g * N_HPAIRS + pr + 1) * PW))
             for g in range(NSA_KV_GROUPS) for pr in range(N_HPAIRS)]

    def keys_of(br, t):
        k0 = pl.multiple_of(t * TK, TK)
        return [kx_s[br, pl.ds(k0, TK), g * KX:(g + 1) * KX] for g in range(NSA_KV_GROUPS)]

    def values_of(br, t):
        return [jnp.concatenate([vt_s[br, t * (TK // KT) + u, g] for u in range(TK // KT)], axis=1)
                for g in range(NSA_KV_GROUPS)]

    def finish_unit(br, v_tt, g, cols, alpha):
        acc_s[br, :, cols] = alpha * acc_s[br, :, cols] + _dot(v_tt[g], p_s[:, cols])

    def finish(br, t, alphas):
        v_tt = values_of(br, t)
        for (g, cols), alpha in zip(units, alphas):
            finish_unit(br, v_tt, g, cols, alpha)

    def tile_step(u, carry, edge, prefetch=True):
        alphas_prev, br_prev, t_prev = carry
        br, t = tile_of(u)
        if prefetch:
            k_next = keys_of(*tile_of(u + 1))
        v_prev = values_of(br_prev, t_prev)
        if edge:
            rel = rel0 + t * TK
            ok = (rel <= 0) & (rel > jnp.where(br == BR_WIN, -WINDOW, -(1 << 30)))
            bias = jnp.where(ok, 0.0, NEG_INF)
            bias = jnp.concatenate([bias, bias], axis=1)
        alphas = []
        for (g, cols), alpha_prev in zip(units, alphas_prev):
            s2 = s_s[:, cols]
            if edge:
                s2 = s2 + bias
            if prefetch:
                s_s[:, cols] = _dot(k_next[g], qt_s[:, cols])
            finish_unit(br_prev, v_prev, g, cols, alpha_prev)
            m_prev = m_s[br, :, cols]
            m_new = jnp.maximum(m_prev, jnp.max(s2, axis=0, keepdims=True))
            m_s[br, :, cols] = m_new
            p_s[:, cols] = jnp.exp2(s2 - m_new).astype(BF16)
            alphas.append(jnp.exp2(m_prev - m_new))
        return tuple(alphas), br, t

    k_first = keys_of(*tile_of(0))
    for g, cols in units:
        s_s[:, cols] = _dot(k_first[g], qt_s[:, cols])
    p_s[...] = jnp.zeros_like(p_s)
    carry = (tuple(jnp.ones((1, PW), F32) for _ in units), jnp.int32(BR_SEL), jnp.int32(0))
    carry = lax.fori_loop(0, n_plain, lambda u, c: tile_step(u, c, edge=False), carry)
    carry = lax.fori_loop(n_plain, n_tiles - 1, lambda u, c: tile_step(u, c, edge=True), carry)
    alphas_last, br_last, t_last = tile_step(n_tiles - 1, carry, edge=True, prefetch=False)
    finish(br_last, t_last, alphas_last)

    for g, cols in units:
        heads = [cols.start // QB + u for u in range(2)]

        def gate_row(branch):
            return jnp.concatenate([gates_t[T_GATE + h * N_NSA_BRANCHES + branch:
                                            T_GATE + h * N_NSA_BRANCHES + branch + 1, :] for h in heads], axis=1)

        o_t = (gate_row(0) * oc_s[:, cols]
               + (gate_row(1) / acc_s[BR_SEL, hd:hd + 1, cols]) * acc_s[BR_SEL, 0:hd, cols]
               + (gate_row(2) / acc_s[BR_WIN, hd:hd + 1, cols]) * acc_s[BR_WIN, 0:hd, cols])
        for u, h in enumerate(heads):
            zz = z_ref[:, h * hd:(h + 1) * hd].astype(F32)
            o_ref[:, h * hd:(h + 1) * hd] = (o_t[:, u * QB:(u + 1) * QB].T * _silu(zz)).astype(BF16)


def _nsa(proj, tail, kc, vc, qnw, ksnw, kwnw, selmap_t, batch, seq):
    nq = seq // QB
    hd = NSA_HEAD_DIM
    m = batch * seq

    def full(shape):
        return pl.BlockSpec(shape, lambda b, i: (0,) * len(shape))

    return pl.pallas_call(
        _nsa_kernel,
        grid=(batch, nq),
        in_specs=[
            pl.BlockSpec((QB, NSA_D), lambda b, i: (b * nq + i, C_Q // NSA_D)),
            pl.BlockSpec((seq, 2 * NSA_KV_D), lambda b, i: (b, C_KVSLC // (2 * NSA_KV_D))),
            pl.BlockSpec((seq, 2 * NSA_KV_D), lambda b, i: (b, C_KVWIN // (2 * NSA_KV_D))),
            pl.BlockSpec((1, N_CMP_PAD, NSA_KV_D), lambda b, i: (b, 0, 0)),
            pl.BlockSpec((1, NSA_KV_D, N_CMP_PAD), lambda b, i: (b, 0, 0)),
            pl.BlockSpec((QB, NSA_D), lambda b, i: (b * nq + i, C_ZNSA // NSA_D)),
            pl.BlockSpec((QB, LANES), lambda b, i: (b * nq + i, 0)),
            full((1, hd)), full((1, hd)), full((1, hd)),
            full((N_SLC_PAD, N_CMP_PAD)),
        ],
        out_specs=pl.BlockSpec((QB, NSA_D), lambda b, i: (b * nq + i, 0)),
        out_shape=jax.ShapeDtypeStruct((m, NSA_D), BF16),
        scratch_shapes=[
            pltpu.VMEM((2, seq, NSA_KV_GROUPS * KX), BF16),
            pltpu.VMEM((2, seq // KT, NSA_KV_GROUPS, VT_ROWS, KT), BF16),
            pltpu.VMEM((KX, NSA_HEADS * QB), BF16),
            pltpu.VMEM((hd, NSA_HEADS * QB), F32),
            pltpu.VMEM((2, 1, NSA_HEADS * QB), F32),
            pltpu.VMEM((2, VT_ROWS, NSA_HEADS * QB), F32),
            pltpu.VMEM((TK, NSA_HEADS * QB), F32),
            pltpu.VMEM((TK, NSA_HEADS * QB), BF16),
        ],
        compiler_params=_params(("arbitrary", "arbitrary")),
    )(proj, proj, proj, kc, vc, proj, tail, qnw, ksnw, kwnw, selmap_t)


OUT_TM = 1024
OUT_TN = 512


def _merge_kernel(ys_ref, yn_ref, gs_ref, gn_ref, ws_ref, wn_ref, o_ref):
    a = _dot(ys_ref[...], ws_ref[...])
    b = _dot(yn_ref[...], wn_ref[...])
    merged = _sigmoid(gs_ref[...].astype(F32)) * a + _sigmoid(gn_ref[...].astype(F32)) * b
    o_ref[...] = merged.astype(BF16)


def _merge(y_ssd, y_nsa, proj, w_s, w_n):
    m = y_ssd.shape[0]
    nb = D_MODEL // OUT_TN
    return pl.pallas_call(
        _merge_kernel,
        grid=(nb, m // OUT_TM),
        in_specs=[
            pl.BlockSpec((OUT_TM, D_MODEL), lambda j, i: (i, 0)),
            pl.BlockSpec((OUT_TM, D_MODEL), lambda j, i: (i, 0)),
            pl.BlockSpec((OUT_TM, OUT_TN), lambda j, i: (i, C_GLSSD // OUT_TN + j)),
            pl.BlockSpec((OUT_TM, OUT_TN), lambda j, i: (i, C_GLNSA // OUT_TN + j)),
            pl.BlockSpec((D_MODEL, OUT_TN), lambda j, i: (0, j)),
            pl.BlockSpec((D_MODEL, OUT_TN), lambda j, i: (0, j)),
        ],
        out_specs=pl.BlockSpec((OUT_TM, OUT_TN), lambda j, i: (i, j)),
        out_shape=jax.ShapeDtypeStruct((m, D_MODEL), BF16),
        compiler_params=_params(("arbitrary", "arbitrary")),
    )(y_ssd, y_nsa, proj, proj, w_s, w_n)


def _final_kernel(m_ref, x_ref, w_ref, o_ref):
    o_ref[...] = x_ref[...] + _dot(m_ref[...], w_ref[...])


def _final(merged, x2, w_o):
    m = merged.shape[0]
    nb = D_MODEL // OUT_TN
    return pl.pallas_call(
        _final_kernel,
        grid=(nb, m // OUT_TM),
        in_specs=[
            pl.BlockSpec((OUT_TM, D_MODEL), lambda j, i: (i, 0)),
            pl.BlockSpec((OUT_TM, OUT_TN), lambda j, i: (i, j)),
            pl.BlockSpec((D_MODEL, OUT_TN), lambda j, i: (0, j)),
        ],
        out_specs=pl.BlockSpec((OUT_TM, OUT_TN), lambda j, i: (i, j)),
        out_shape=jax.ShapeDtypeStruct((m, D_MODEL), F32),
        compiler_params=_params(("arbitrary", "arbitrary")),
    )(merged, x2, w_o)


def _w_in_sections(w_in):
    sec = {name: w_in[:, _OFFS[i]:_OFFS[i + 1]] for i, name in enumerate(
        ("z_ssd", "xbc", "dt", "q", "k_cmp", "v_cmp", "k_slc", "v_slc", "k_win", "v_win",
         "z_nsa", "gates", "gl_ssd", "gl_nsa"))}
    main = jnp.concatenate([
        sec["z_ssd"], sec["xbc"][:, :SSD_D_INNER], sec["q"], sec["z_nsa"], sec["gl_ssd"], sec["gl_nsa"],
        sec["xbc"][:, SSD_D_INNER:], sec["k_cmp"], sec["v_cmp"], sec["k_slc"], sec["v_slc"],
        sec["k_win"], sec["v_win"]], axis=1).astype(BF16)
    pad = LANES - SSD_HEADS - NSA_HEADS * N_NSA_BRANCHES
    tail = jnp.concatenate([sec["dt"], sec["gates"], jnp.zeros((D_MODEL, pad), w_in.dtype)], axis=1).astype(BF16)
    return main, tail


def _constants():
    tri = np.tril(np.ones((CQ, CQ), np.float32))
    expand = np.zeros((LANES, SSD_D_INNER), np.float32)
    for h in range(SSD_HEADS):
        expand[h, h * SSD_HEAD_DIM:(h + 1) * SSD_HEAD_DIM] = 1.0
    n_cmp = N_CMP_PAD - 1
    ci = np.arange(n_cmp)[:, None] * CMP_STRIDE
    sj = np.arange(32)[None, :] * SEL_BLOCK
    sel_map = ((ci < sj + SEL_BLOCK) & (ci + CMP_BLOCK > sj)).astype(np.float32)
    selmap_t = np.zeros((N_SLC_PAD, N_CMP_PAD), np.float32)
    selmap_t[:32, :n_cmp] = sel_map.T
    shift = np.zeros(((SSD_CONV_K - 1) * CQ, CONV_HIST + CQ), np.float32)
    for k in range(SSD_CONV_K - 1):
        for t in range(CQ):
            shift[k * CQ + t, CONV_HIST + t - (SSD_CONV_K - 1) + k] = 1.0
    return jnp.asarray(tri, BF16), jnp.asarray(expand, BF16), jnp.asarray(selmap_t, BF16), jnp.asarray(shift, BF16)


def _pad_lanes(v):
    return jnp.concatenate([v.astype(F32), jnp.zeros((LANES - v.shape[0],), F32)])[None, :]


def kernel(x, norm_w, w_in, conv_w, conv_b, dt_bias, a_log, d_skip, ssd_norm_w, q_norm_w, k_cmp_norm_w,
           k_slc_norm_w, k_win_norm_w, cmp_pe_k, cmp_w1_k, cmp_b1_k, cmp_w2_k, cmp_pe_v, cmp_w1_v, cmp_b1_v,
           cmp_w2_v, w_out_ssd, w_out_nsa, w_o):
    batch, seq, _ = x.shape
    assert seq == 2048 and (batch * seq) % IN_TM == 0
    x2 = x.reshape(batch * seq, D_MODEL)
    w_main, w_tail = _w_in_sections(w_in)
    tri, expand, selmap_t, shift = _constants()

    proj, tail = _inproj(x2, norm_w[None, :], w_main, w_tail)

    y_ssd = _ssd(proj, tail,
                 conv_w[:, :SSD_D_INNER], conv_b[None, :SSD_D_INNER],
                 conv_w[:, SSD_D_INNER:], conv_b[None, SSD_D_INNER:],
                 _pad_lanes(dt_bias), _pad_lanes(a_log),
                 jnp.repeat(d_skip, SSD_HEAD_DIM)[None, :], ssd_norm_w[None, :],
                 tri, expand, shift, batch, seq)

    kc, vc = _compress(proj, cmp_pe_k, cmp_w1_k.astype(BF16), cmp_b1_k[None, :], cmp_w2_k.astype(BF16),
                       cmp_pe_v, cmp_w1_v.astype(BF16), cmp_b1_v[None, :], cmp_w2_v.astype(BF16),
                       k_cmp_norm_w[None, :], batch, seq)

    y_nsa = _nsa(proj, tail, kc, vc, q_norm_w[None, :], k_slc_norm_w[None, :], k_win_norm_w[None, :],
                 selmap_t, batch, seq)

    merged = _merge(y_ssd, y_nsa, proj, w_out_ssd.astype(BF16), w_out_nsa.astype(BF16))
    out = _final(merged, x2, w_o.astype(BF16))
    return out.reshape(batch, seq, D_MODEL)
```

```python
import functools

import numpy as np
import jax
import jax.numpy as jnp
from jax import lax
from jax.experimental import pallas as pl
from jax.experimental.pallas import tpu as pltpu

D_MODEL = 2048
SSD_D_INNER = D_MODEL
SSD_HEAD_DIM = 64
SSD_HEADS = SSD_D_INNER // SSD_HEAD_DIM
SSD_GROUPS = 4
SSD_D_STATE = 128
SSD_CONV_K = 4
SSD_CHUNK = 128
SSD_BC_DIM = 2 * SSD_GROUPS * SSD_D_STATE
SSD_CONV_DIM = SSD_D_INNER + SSD_BC_DIM

NSA_HEADS = 16
NSA_HEAD_DIM = 128
NSA_KV_GROUPS = 2
NSA_HPG = NSA_HEADS // NSA_KV_GROUPS
NSA_D = NSA_HEADS * NSA_HEAD_DIM
NSA_KV_D = NSA_KV_GROUPS * NSA_HEAD_DIM
N_NSA_BRANCHES = 3
CMP_BLOCK = 32
CMP_STRIDE = 16
SEL_BLOCK = 64
SEL_TOPK = 8
WINDOW = 512
Q_BLOCK = 128

EPS = 1e-6
NEG_INF = -1e30
FORCED_SCORE = 1e9

LANES = 128
SUBLANES = 8
VMEM_LIMIT = 56 * 1024 * 1024

F32 = jnp.float32
BF16 = jnp.bfloat16
HIGHEST = lax.Precision.HIGHEST

C_ZSSD = 0
C_XS = 2048
C_Q = 4096
C_ZNSA = 6144
C_GLSSD = 8192
C_GLNSA = 10240
C_BC = 12288
C_KVCMP = 13312
C_KVSLC = 13824
C_KVWIN = 14336
N_MAIN = 14848
T_GATE = SSD_HEADS

_SIZES = [SSD_D_INNER, SSD_CONV_DIM, SSD_HEADS, NSA_D] + [NSA_KV_D] * 6 + [NSA_D, NSA_HEADS * N_NSA_BRANCHES, D_MODEL, D_MODEL]
_OFFS = [0] + [int(o) for o in np.cumsum(_SIZES)]


def _sigmoid(v):
    return 0.5 + 0.5 * jnp.tanh(0.5 * v)


def _silu(v):
    h = 0.5 * v
    return h + h * jnp.tanh(h)


def _softplus(v):
    return jnp.maximum(v, 0.0) + jnp.log(1.0 + jnp.exp(-jnp.abs(v)))


def _rms(v, w):
    ms = jnp.mean(v * v, axis=-1, keepdims=True)
    return v * lax.rsqrt(ms + EPS) * w


def _dot(a, b, precision=None):
    return jnp.dot(a, b, preferred_element_type=F32, precision=precision)


def _split3(v):
    hi = v.astype(BF16)
    r1 = v - hi.astype(F32)
    mid = r1.astype(BF16)
    lo = (r1 - mid.astype(F32)).astype(BF16)
    return hi, mid, lo


def _dot_nt(a, b, precision=None):
    return lax.dot_general(a, b, (((1,), (1,)), ((), ())), preferred_element_type=F32, precision=precision)


def _params(sem):
    return pltpu.CompilerParams(dimension_semantics=sem, vmem_limit_bytes=VMEM_LIMIT)


IN_TM = 1024
IN_TN = 512
IN_RC = 256


def _inproj_kernel(x_ref, nw_ref, w_ref, wt_ref, o_ref, t_ref, h_ref):
    j = pl.program_id(1)

    @pl.when(j == 0)
    def _():
        def body(r, carry):
            r0 = pl.multiple_of(r * IN_RC, IN_RC)
            h_ref[pl.ds(r0, IN_RC), :] = _rms(x_ref[pl.ds(r0, IN_RC), :], nw_ref[...]).astype(BF16)
            return carry

        lax.fori_loop(0, IN_TM // IN_RC, body, 0)
        t_ref[...] = _dot(h_ref[...], wt_ref[...])

    o_ref[...] = _dot(h_ref[...], w_ref[...]).astype(BF16)


def _inproj(x2, norm_w, w_main, w_tail):
    m = x2.shape[0]
    return pl.pallas_call(
        _inproj_kernel,
        grid=(m // IN_TM, N_MAIN // IN_TN),
        in_specs=[
            pl.BlockSpec((IN_TM, D_MODEL), lambda i, j: (i, 0)),
            pl.BlockSpec((1, D_MODEL), lambda i, j: (0, 0)),
            pl.BlockSpec((D_MODEL, IN_TN), lambda i, j: (0, j)),
            pl.BlockSpec((D_MODEL, LANES), lambda i, j: (0, 0)),
        ],
        out_specs=[
            pl.BlockSpec((IN_TM, IN_TN), lambda i, j: (i, j)),
            pl.BlockSpec((IN_TM, LANES), lambda i, j: (i, 0)),
        ],
        out_shape=[
            jax.ShapeDtypeStruct((m, N_MAIN), BF16),
            jax.ShapeDtypeStruct((m, LANES), F32),
        ],
        scratch_shapes=[pltpu.VMEM((IN_TM, D_MODEL), BF16)],
        compiler_params=_params(("arbitrary", "arbitrary")),
    )(x2, norm_w, w_main, w_tail)


CQ = SSD_CHUNK
PAIR = 2 * SSD_HEAD_DIM
N_PAIRS = SSD_HEADS // 2
PAIRS_PER_GROUP = N_PAIRS // SSD_GROUPS
CONV_HIST = 16
CONV_COLS = 256


def _ssd_kernel(z_ref, xs_ref, bc_ref, tail_ref, cwx_ref, cbx_ref, cwb_ref, cbb_ref, dtb_ref, alog_ref,
                dsk_ref, nw_ref, tri_ref, exp_ref, shift_ref, o_ref, xbuf, bbuf, xa_ref, ba_ref, st_ref, y_ref):
    c = pl.program_id(1)

    @pl.when(c == 0)
    def _():
        xbuf[0:CONV_HIST, :] = jnp.zeros((CONV_HIST, SSD_D_INNER), BF16)
        bbuf[0:CONV_HIST, :] = jnp.zeros((CONV_HIST, SSD_BC_DIM), BF16)
        st_ref[...] = jnp.zeros_like(st_ref)

    xbuf[CONV_HIST:CONV_HIST + CQ, :] = xs_ref[...]
    bbuf[CONV_HIST:CONV_HIST + CQ, :] = bc_ref[...]

    def conv(buf, cur_ref, w_ref, b_ref, out_ref):
        for j in range(buf.shape[1] // CONV_COLS):
            cs = slice(j * CONV_COLS, (j + 1) * CONV_COLS)
            shifted = _dot(shift_ref[...], buf[:, cs])
            acc = b_ref[:, cs] + cur_ref[:, cs].astype(F32) * w_ref[SSD_CONV_K - 1:SSD_CONV_K, cs]
            for k in range(SSD_CONV_K - 1):
                acc = acc + shifted[k * CQ:(k + 1) * CQ, :] * w_ref[k:k + 1, cs]
            out_ref[:, cs] = _silu(acc)

    conv(xbuf, xs_ref, cwx_ref, cbx_ref, xa_ref)
    conv(bbuf, bc_ref, cwb_ref, cbb_ref, ba_ref)
    xbuf[0:CONV_HIST, :] = xbuf[CQ:CQ + CONV_HIST, :]
    bbuf[0:CONV_HIST, :] = bbuf[CQ:CQ + CONV_HIST, :]

    dt = _softplus(tail_ref[...] + dtb_ref[...])
    la = dt * (-jnp.exp(alog_ref[...]))
    lc = sum(_dot(tri_ref[...], part) for part in _split3(la)) * LOG2E
    lc_t = lc.T
    dt_t = dt.T
    w_t = jnp.exp2(lc_t[:, CQ - 1:CQ] - lc_t) * dt_t
    cd = jnp.exp2(lc[CQ - 1:CQ, :])
    cdx = sum(_dot(part, exp_ref[...]) for part in _split3(jnp.broadcast_to(cd, (SUBLANES, LANES))))

    row = lax.broadcasted_iota(jnp.int32, (CQ, CQ), 0)
    col = lax.broadcasted_iota(jnp.int32, (CQ, CQ), 1)
    causal = row >= col
    lo_half = col < SSD_HEAD_DIM

    for g in range(SSD_GROUPS):
        b_f = ba_ref[:, g * SSD_D_STATE:(g + 1) * SSD_D_STATE]
        c_f = ba_ref[:, (SSD_GROUPS + g) * SSD_D_STATE:(SSD_GROUPS + g + 1) * SSD_D_STATE]
        cb = _dot_nt(c_f.astype(BF16), b_f.astype(BF16))
        b_t = b_f.T
        for pp in range(PAIRS_PER_GROUP):
            p = g * PAIRS_PER_GROUP + pp
            lanes = slice(p * PAIR, (p + 1) * PAIR)
            xp = xa_ref[:, lanes].astype(BF16)
            s_prev = st_ref[:, lanes]
            s_prev_b = s_prev.astype(BF16)
            x_half = (jnp.where(lo_half, xp, 0.0), jnp.where(lo_half, 0.0, xp))
            s_half = (jnp.where(lo_half, s_prev_b, 0.0), jnp.where(lo_half, 0.0, s_prev_b))
            y_pair, s_new = None, None
            for hh in range(2):
                h = 2 * p + hh
                lcol = jnp.broadcast_to(lc[:, h:h + 1], (CQ, CQ))
                lmat = jnp.exp2(jnp.where(causal, lcol - lc_t[h:h + 1, :], -jnp.inf))
                m_h = (cb * lmat * dt_t[h:h + 1, :]).astype(BF16)
                c_s = (c_f * jnp.exp2(lcol)).astype(BF16)
                y_h = _dot(m_h, x_half[hh]) + _dot(c_s, s_half[hh])
                w_h = (b_t * w_t[h:h + 1, :]).astype(BF16)
                s_h = _dot(w_h, x_half[hh])
                y_pair = y_h if y_pair is None else y_pair + y_h
                s_new = s_h if s_new is None else s_new + s_h
            y_ref[:, lanes] = y_pair
            st_ref[:, lanes] = s_prev * cdx[0:1, lanes] + s_new

    y = y_ref[...] + xa_ref[...] * dsk_ref[...]
    y = y * _silu(z_ref[...].astype(F32))
    o_ref[...] = _rms(y, nw_ref[...]).astype(BF16)


def _ssd(proj, tail, cwx, cbx, cwb, cbb, dtb, alog, dsk, nw, tri, expand, shift, batch, seq):
    nc = seq // CQ
    m = batch * seq

    def full(shape):
        return pl.BlockSpec(shape, lambda b, c: (0,) * len(shape))

    return pl.pallas_call(
        _ssd_kernel,
        grid=(batch, nc),
        in_specs=[
            pl.BlockSpec((CQ, SSD_D_INNER), lambda b, c: (b * nc + c, C_ZSSD // SSD_D_INNER)),
            pl.BlockSpec((CQ, SSD_D_INNER), lambda b, c: (b * nc + c, C_XS // SSD_D_INNER)),
            pl.BlockSpec((CQ, SSD_BC_DIM), lambda b, c: (b * nc + c, C_BC // SSD_BC_DIM)),
            pl.BlockSpec((CQ, LANES), lambda b, c: (b * nc + c, 0)),
            full((SSD_CONV_K, SSD_D_INNER)), full((1, SSD_D_INNER)),
            full((SSD_CONV_K, SSD_BC_DIM)), full((1, SSD_BC_DIM)),
            full((1, LANES)), full((1, LANES)),
            full((1, SSD_D_INNER)), full((1, SSD_D_INNER)),
            full((CQ, CQ)), full((LANES, SSD_D_INNER)), full(((SSD_CONV_K - 1) * CQ, CONV_HIST + CQ)),
        ],
        out_specs=pl.BlockSpec((CQ, SSD_D_INNER), lambda b, c: (b * nc + c, 0)),
        out_shape=jax.ShapeDtypeStruct((m, SSD_D_INNER), BF16),
        scratch_shapes=[
            pltpu.VMEM((CONV_HIST + CQ, SSD_D_INNER), BF16),
            pltpu.VMEM((CONV_HIST + CQ, SSD_BC_DIM), BF16),
            pltpu.VMEM((CQ, SSD_D_INNER), F32),
            pltpu.VMEM((CQ, SSD_BC_DIM), F32),
            pltpu.VMEM((SSD_D_STATE, SSD_D_INNER), F32),
            pltpu.VMEM((CQ, SSD_D_INNER), F32),
        ],
        compiler_params=_params(("arbitrary", "arbitrary")),
    )(proj, proj, proj, tail, cwx, cbx, cwb, cbb, dtb, alog, dsk, nw, tri, expand, shift)


N_CMP_PAD = 128
CMP_HALF = CMP_BLOCK // CMP_STRIDE


def _cmp_kernel(kv_ref, pek_ref, w1k_ref, b1k_ref, w2k_ref, pev_ref, w1v_ref, b1v_ref, w2v_ref, knw_ref,
                kc_ref, vc_ref, buf):
    hd = NSA_HEAD_DIM
    for slab in range(2 * NSA_KV_GROUPS):
        buf[slab] = kv_ref[:, slab * hd:(slab + 1) * hd].astype(F32)
    for kind, (pe_ref, w1_ref, b1_ref, w2_ref, out_ref) in enumerate((
            (pek_ref, w1k_ref, b1k_ref, w2k_ref, kc_ref), (pev_ref, w1v_ref, b1v_ref, w2v_ref, vc_ref))):
        for g in range(NSA_KV_GROUPS):
            slab = kind * NSA_KV_GROUPS + g
            u = [jnp.zeros((N_CMP_PAD, hd), F32) for _ in range(CMP_HALF)]
            for r in range(CMP_STRIDE):
                xr = buf[slab, pl.ds(r, N_CMP_PAD, stride=CMP_STRIDE), :]
                for a in range(CMP_HALF):
                    l = a * CMP_STRIDE + r
                    u[a] = u[a] + _dot((xr + pe_ref[l:l + 1, :]).astype(BF16), w1_ref[l * hd:(l + 1) * hd, :])
            acc = u[0] + pltpu.roll(u[1], N_CMP_PAD - 1, 0) + b1_ref[...]
            out = _dot(_silu(acc).astype(BF16), w2_ref[...])
            if kind == 0:
                out_ref[0, :, g * hd:(g + 1) * hd] = _rms(out, knw_ref[...]).astype(BF16)
            else:
                out_ref[0, g * hd:(g + 1) * hd, :] = out.T.astype(BF16)


def _compress(proj, pek, w1k, b1k, w2k, pev, w1v, b1v, w2v, knw, batch, seq):
    hd = NSA_HEAD_DIM

    def full(shape):
        return pl.BlockSpec(shape, lambda b: (0,) * len(shape))

    return pl.pallas_call(
        _cmp_kernel,
        grid=(batch,),
        in_specs=[
            pl.BlockSpec((seq, 2 * NSA_KV_D), lambda b: (b, C_KVCMP // (2 * NSA_KV_D))),
            full((CMP_BLOCK, hd)), full((CMP_BLOCK * hd, hd)), full((1, hd)), full((hd, hd)),
            full((CMP_BLOCK, hd)), full((CMP_BLOCK * hd, hd)), full((1, hd)), full((hd, hd)),
            full((1, hd)),
        ],
        out_specs=[pl.BlockSpec((1, N_CMP_PAD, NSA_KV_D), lambda b: (b, 0, 0)),
                   pl.BlockSpec((1, NSA_KV_D, N_CMP_PAD), lambda b: (b, 0, 0))],
        out_shape=[jax.ShapeDtypeStruct((batch, N_CMP_PAD, NSA_KV_D), BF16),
                   jax.ShapeDtypeStruct((batch, NSA_KV_D, N_CMP_PAD), BF16)],
        scratch_shapes=[pltpu.VMEM((2 * NSA_KV_GROUPS, seq, hd), F32)],
        compiler_params=_params(("arbitrary",)),
    )(proj, pek, w1k, b1k, w2k, pev, w1v, b1v, w2v, knw)


QB = Q_BLOCK
GROWS = NSA_HPG * QB
TK = 256
BR_SEL, BR_WIN = 0, 1
N_SLC = 32
N_SLC_PAD = 128
NORM_RC = 256
KT = 128
PW = 2 * QB
N_HPAIRS = NSA_HPG // 2
KX = 2 * NSA_HEAD_DIM
VT_ROWS = NSA_HEAD_DIM + 16
LOG2E = 1.4426950408889634


def _nsa_kernel(q_ref, kvs_ref, kvw_ref, kc_ref, vct_ref, z_ref, tail_ref, qnw_ref, ksnw_ref,
                kwnw_ref, selmap_ref, o_ref, kx_s, vt_s, qt_s, oc_s, gt_s, m_s, acc_s, s_s, p_s):
    i = pl.program_id(1)
    qs = i * QB
    hd = NSA_HEAD_DIM
    seq = kvs_ref.shape[0]

    def prepare(q_r, tail_r, q0):
        qw = qnw_ref[...] * (hd ** -0.5 * LOG2E)
        for h in range(NSA_HEADS):
            qh = q_r[:, h * hd:(h + 1) * hd].astype(F32)
            qt_s[0:hd, h * QB:(h + 1) * QB] = _rms(qh, qw).T.astype(BF16)
        gt_s[...] = _sigmoid(tail_r[...]).T

        c_idx = lax.broadcasted_iota(jnp.int32, (N_CMP_PAD, GROWS), 0)
        q_in = lax.broadcasted_iota(jnp.int32, (N_CMP_PAD, GROWS), 1) & (QB - 1)
        mask_c = c_idx * CMP_STRIDE + (CMP_BLOCK - 1) <= q0 + q_in
        any_c = CMP_BLOCK - 1 <= q0 + q_in[0:1, :]
        jj = lax.broadcasted_iota(jnp.int32, (N_SLC, QB), 0)
        tq_t = q0 + lax.broadcasted_iota(jnp.int32, (N_SLC, QB), 1)

        for g in range(NSA_KV_GROUPS):
            gl = slice(g * hd, (g + 1) * hd)
            gcols = slice(g * GROWS, (g + 1) * GROWS)
            s = _dot(kc_ref[0, :, gl], qt_s[0:hd, gcols])
            s = jnp.where(mask_c, s, NEG_INF)
            e = jnp.exp2(s - jnp.max(s, axis=0, keepdims=True))
            p_cmp = e * jnp.where(any_c, 1.0 / jnp.sum(e, axis=0, keepdims=True), 0.0)
            oc_s[:, gcols] = _dot(vct_ref[0, gl, :], p_cmp.astype(BF16))

            p_sum = p_cmp[:, 0:QB]
            for k in range(1, NSA_HPG):
                p_sum = p_sum + p_cmp[:, k * QB:(k + 1) * QB]
            imp = sum(_dot(selmap_ref[...], part) for part in _split3(p_sum))[0:N_SLC, :]
            forced = (jj == (tq_t >> 6)) | (jj == 0)
            imp = jnp.where(forced, FORCED_SCORE, jnp.where(jj * SEL_BLOCK <= tq_t, imp, -1.0))
            rows = [slice(r, r + SUBLANES) for r in range(0, N_SLC, SUBLANES)]
            ranks = [jnp.zeros((SUBLANES, QB), jnp.int32) for _ in rows]
            for j2 in range(N_SLC):
                other = imp[j2:j2 + 1, :]
                for r, rs in enumerate(rows):
                    if rs.stop <= j2:
                        beats = other > imp[rs, :]
                    elif rs.start > j2:
                        beats = other >= imp[rs, :]
                    else:
                        beats = (other > imp[rs, :]) | ((other == imp[rs, :]) & (jj[rs, :] > j2))
                    ranks[r] = ranks[r] + jnp.where(beats, 1, 0)
            rank = jnp.concatenate(ranks, axis=0)
            picked = (rank < SEL_TOPK) & (jj * SEL_BLOCK <= tq_t)
            sel_bias = jnp.concatenate([jnp.where(picked, 0.0, NEG_INF).astype(F32),
                                        jnp.zeros((hd - N_SLC, QB), F32)], axis=0).astype(BF16)
            for k in range(NSA_HPG):
                qt_s[hd:KX, (g * NSA_HPG + k) * QB:(g * NSA_HPG + k + 1) * QB] = sel_bias

    @pl.when(i == 0)
    def _():
        for br, src, nw in ((BR_SEL, kvs_ref, ksnw_ref), (BR_WIN, kvw_ref, kwnw_ref)):
            for g in range(NSA_KV_GROUPS):
                def body(r, carry, br=br, src=src, nw=nw, g=g):
                    r0 = pl.multiple_of(r * NORM_RC, NORM_RC)
                    kk = src[pl.ds(r0, NORM_RC), g * hd:(g + 1) * hd].astype(F32)
                    kx_s[br, pl.ds(r0, NORM_RC), g * KX:g * KX + hd] = _rms(kk, nw[...]).astype(BF16)
                    blk = (r0 + lax.broadcasted_iota(jnp.int32, (NORM_RC, hd), 0)) >> 6
                    hit = blk == lax.broadcasted_iota(jnp.int32, (NORM_RC, hd), 1)
                    aux = jnp.where(hit, 1.0 if br == BR_SEL else 0.0, 0.0)
                    kx_s[br, pl.ds(r0, NORM_RC), g * KX + hd:(g + 1) * KX] = aux.astype(BF16)
                    return carry

                lax.fori_loop(0, seq // NORM_RC, body, 0)

                def body_t(r, carry, br=br, src=src, g=g):
                    r0 = pl.multiple_of(r * KT, KT)
                    vv = src[pl.ds(r0, KT), NSA_KV_D + g * hd:NSA_KV_D + (g + 1) * hd].astype(F32)
                    vt_s[br, r, g, 0:hd, :] = vv.T.astype(BF16)
                    vt_s[br, r, g, hd:VT_ROWS, :] = jnp.ones((VT_ROWS - hd, KT), BF16)
                    return carry

                lax.fori_loop(0, seq // KT, body_t, 0)

    prepare(q_ref, tail_ref, qs)

    n_sel = (qs + QB + TK - 1) // TK
    win_first = jnp.maximum(qs - (WINDOW - 1), 0) // TK
    n_win_mid = jnp.maximum(n_sel - win_first - 2, 0)
    n_plain = n_sel - 1 + n_win_mid
    n_tiles = n_plain + 2 + jnp.where(n_sel - win_first >= 2, 1, 0)

    def tile_of(u):
        e = u - n_plain
        br = jnp.where(u < n_plain, jnp.where(u < n_sel - 1, BR_SEL, BR_WIN), jnp.where(e == 0, BR_SEL, BR_WIN))
        t = jnp.where(u < n_plain, jnp.where(u < n_sel - 1, u, win_first + 1 + (u - (n_sel - 1))),
                      jnp.where(e == 1, win_first, n_sel - 1))
        return br, t

    rel0 = (lax.broadcasted_iota(jnp.int32, (TK, QB), 0) - lax.broadcasted_iota(jnp.int32, (TK, QB), 1)) - qs

    m_s[...] = jnp.full_like(m_s, NEG_INF)
    acc_s[...] = jnp.zeros_like(acc_s)
    units = [(g, slice((g * N_HPAIRS + pr) * PW, (g * N_HPAIRS + pr + 1) * PW))
             for g in range(NSA_KV_GROUPS) for pr in range(N_HPAIRS)]

    def keys_of(br, t):
        k0 = pl.multiple_of(t * TK, TK)
        return [kx_s[br, pl.ds(k0, TK), g * KX:(g + 1) * KX] for g in range(NSA_KV_GROUPS)]

    def values_of(br, t):
        return [jnp.concatenate([vt_s[br, t * (TK // KT) + u, g] for u in range(TK // KT)], axis=1)
                for g in range(NSA_KV_GROUPS)]

    def finish_unit(br, v_tt, g, cols, alpha):
        acc_s[br, :, cols] = alpha * acc_s[br, :, cols] + _dot(v_tt[g], p_s[:, cols])

    def finish(br, t, alphas):
        v_tt = values_of(br, t)
        for (g, cols), alpha in zip(units, alphas):
            finish_unit(br, v_tt, g, cols, alpha)

    def tile_step(u, carry, edge, prefetch=True):
        alphas_prev, br_prev, t_prev = carry
        br, t = tile_of(u)
        if prefetch:
            k_next = keys_of(*tile_of(u + 1))
        v_prev = values_of(br_prev, t_prev)
        if edge:
            rel = rel0 + t * TK
            ok = (rel <= 0) & (rel > jnp.where(br == BR_WIN, -WINDOW, -(1 << 30)))
            bias = jnp.where(ok, 0.0, NEG_INF)
            bias = jnp.concatenate([bias, bias], axis=1)
        alphas = []
        for (g, cols), alpha_prev in zip(units, alphas_prev):
            s2 = s_s[:, cols]
            if edge:
                s2 = s2 + bias
            if prefetch:
                s_s[:, cols] = _dot(k_next[g], qt_s[:, cols])
            finish_unit(br_prev, v_prev, g, cols, alpha_prev)
            m_prev = m_s[br, :, cols]
            m_new = jnp.maximum(m_prev, jnp.max(s2, axis=0, keepdims=True))
            m_s[br, :, cols] = m_new
            p_s[:, cols] = jnp.exp2(s2 - m_new).astype(BF16)
            alphas.append(jnp.exp2(m_prev - m_new))
        return tuple(alphas), br, t

    k_first = keys_of(*tile_of(0))
    for g, cols in units:
        s_s[:, cols] = _dot(k_first[g], qt_s[:, cols])
    p_s[...] = jnp.zeros_like(p_s)
    carry = (tuple(jnp.ones((1, PW), F32) for _ in units), jnp.int32(BR_SEL), jnp.int32(0))
    carry = lax.fori_loop(0, n_plain, lambda u, c: tile_step(u, c, edge=False), carry)
    carry = lax.fori_loop(n_plain, n_tiles - 1, lambda u, c: tile_step(u, c, edge=True), carry)
    alphas_last, br_last, t_last = tile_step(n_tiles - 1, carry, edge=True, prefetch=False)
    finish(br_last, t_last, alphas_last)

    for g, cols in units:
        heads = [cols.start // QB + u for u in range(2)]

        def gate_row(branch):
            return jnp.concatenate([gt_s[T_GATE + h * N_NSA_BRANCHES + branch:
                                         T_GATE + h * N_NSA_BRANCHES + branch + 1, :] for h in heads], axis=1)

        o_t = (gate_row(0) * oc_s[:, cols]
               + (gate_row(1) / acc_s[BR_SEL, hd:hd + 1, cols]) * acc_s[BR_SEL, 0:hd, cols]
               + (gate_row(2) / acc_s[BR_WIN, hd:hd + 1, cols]) * acc_s[BR_WIN, 0:hd, cols])
        for u, h in enumerate(heads):
            zz = z_ref[:, h * hd:(h + 1) * hd].astype(F32)
            o_ref[:, h * hd:(h + 1) * hd] = (o_t[:, u * QB:(u + 1) * QB].T * _silu(zz)).astype(BF16)


def _nsa(proj, tail, kc, vc, qnw, ksnw, kwnw, selmap_t, batch, seq):
    nq = seq // QB
    hd = NSA_HEAD_DIM
    m = batch * seq

    def full(shape):
        return pl.BlockSpec(shape, lambda b, i: (0,) * len(shape))

    return pl.pallas_call(
        _nsa_kernel,
        grid=(batch, nq),
        in_specs=[
            pl.BlockSpec((QB, NSA_D), lambda b, i: (b * nq + i, C_Q // NSA_D)),
            pl.BlockSpec((seq, 2 * NSA_KV_D), lambda b, i: (b, C_KVSLC // (2 * NSA_KV_D))),
            pl.BlockSpec((seq, 2 * NSA_KV_D), lambda b, i: (b, C_KVWIN // (2 * NSA_KV_D))),
            pl.BlockSpec((1, N_CMP_PAD, NSA_KV_D), lambda b, i: (b, 0, 0)),
            pl.BlockSpec((1, NSA_KV_D, N_CMP_PAD), lambda b, i: (b, 0, 0)),
            pl.BlockSpec((QB, NSA_D), lambda b, i: (b * nq + i, C_ZNSA // NSA_D)),
            pl.BlockSpec((QB, LANES), lambda b, i: (b * nq + i, 0)),
            full((1, hd)), full((1, hd)), full((1, hd)),
            full((N_SLC_PAD, N_CMP_PAD)),
        ],
        out_specs=pl.BlockSpec((QB, NSA_D), lambda b, i: (b * nq + i, 0)),
        out_shape=jax.ShapeDtypeStruct((m, NSA_D), BF16),
        scratch_shapes=[
            pltpu.VMEM((2, seq, NSA_KV_GROUPS * KX), BF16),
            pltpu.VMEM((2, seq // KT, NSA_KV_GROUPS, VT_ROWS, KT), BF16),
            pltpu.VMEM((KX, NSA_HEADS * QB), BF16),
            pltpu.VMEM((hd, NSA_HEADS * QB), F32),
            pltpu.VMEM((LANES, QB), F32),
            pltpu.VMEM((2, 1, NSA_HEADS * QB), F32),
            pltpu.VMEM((2, VT_ROWS, NSA_HEADS * QB), F32),
            pltpu.VMEM((TK, NSA_HEADS * QB), F32),
            pltpu.VMEM((TK, NSA_HEADS * QB), BF16),
        ],
        compiler_params=_params(("arbitrary", "arbitrary")),
    )(proj, proj, proj, kc, vc, proj, tail, qnw, ksnw, kwnw, selmap_t)


OUT_TM = 1024
OUT_TN = 512


def _merge_kernel(ys_ref, yn_ref, gs_ref, gn_ref, ws_ref, wn_ref, o_ref):
    a = _dot(ys_ref[...], ws_ref[...])
    b = _dot(yn_ref[...], wn_ref[...])
    merged = _sigmoid(gs_ref[...].astype(F32)) * a + _sigmoid(gn_ref[...].astype(F32)) * b
    o_ref[...] = merged.astype(BF16)


def _merge(y_ssd, y_nsa, proj, w_s, w_n):
    m = y_ssd.shape[0]
    nb = D_MODEL // OUT_TN
    return pl.pallas_call(
        _merge_kernel,
        grid=(nb, m // OUT_TM),
        in_specs=[
            pl.BlockSpec((OUT_TM, D_MODEL), lambda j, i: (i, 0)),
            pl.BlockSpec((OUT_TM, D_MODEL), lambda j, i: (i, 0)),
            pl.BlockSpec((OUT_TM, OUT_TN), lambda j, i: (i, C_GLSSD // OUT_TN + j)),
            pl.BlockSpec((OUT_TM, OUT_TN), lambda j, i: (i, C_GLNSA // OUT_TN + j)),
            pl.BlockSpec((D_MODEL, OUT_TN), lambda j, i: (0, j)),
            pl.BlockSpec((D_MODEL, OUT_TN), lambda j, i: (0, j)),
        ],
        out_specs=pl.BlockSpec((OUT_TM, OUT_TN), lambda j, i: (i, j)),
        out_shape=jax.ShapeDtypeStruct((m, D_MODEL), BF16),
        compiler_params=_params(("arbitrary", "arbitrary")),
    )(y_ssd, y_nsa, proj, proj, w_s, w_n)


FIN_TM = 512


def _final_kernel(m_ref, x_ref, w_ref, o_ref):
    for j in range(D_MODEL // OUT_TN):
        cs = slice(j * OUT_TN, (j + 1) * OUT_TN)
        o_ref[:, cs] = x_ref[:, cs] + _dot(m_ref[...], w_ref[:, cs])


def _final(merged, x2, w_o):
    m = merged.shape[0]
    return pl.pallas_call(
        _final_kernel,
        grid=(m // FIN_TM,),
        in_specs=[
            pl.BlockSpec((FIN_TM, D_MODEL), lambda i: (i, 0)),
            pl.BlockSpec((FIN_TM, D_MODEL), lambda i: (i, 0)),
            pl.BlockSpec((D_MODEL, D_MODEL), lambda i: (0, 0)),
        ],
        out_specs=pl.BlockSpec((FIN_TM, D_MODEL), lambda i: (i, 0)),
        out_shape=jax.ShapeDtypeStruct((m, D_MODEL), F32),
        compiler_params=_params(("arbitrary",)),
    )(merged, x2, w_o)


def _w_in_sections(w_in):
    sec = {name: w_in[:, _OFFS[i]:_OFFS[i + 1]] for i, name in enumerate(
        ("z_ssd", "xbc", "dt", "q", "k_cmp", "v_cmp", "k_slc", "v_slc", "k_win", "v_win",
         "z_nsa", "gates", "gl_ssd", "gl_nsa"))}
    main = jnp.concatenate([
        sec["z_ssd"], sec["xbc"][:, :SSD_D_INNER], sec["q"], sec["z_nsa"], sec["gl_ssd"], sec["gl_nsa"],
        sec["xbc"][:, SSD_D_INNER:], sec["k_cmp"], sec["v_cmp"], sec["k_slc"], sec["v_slc"],
        sec["k_win"], sec["v_win"]], axis=1).astype(BF16)
    pad = LANES - SSD_HEADS - NSA_HEADS * N_NSA_BRANCHES
    tail = jnp.concatenate([sec["dt"], sec["gates"], jnp.zeros((D_MODEL, pad), w_in.dtype)], axis=1).astype(BF16)
    return main, tail


def _constants():
    tri = np.tril(np.ones((CQ, CQ), np.float32))
    expand = np.zeros((LANES, SSD_D_INNER), np.float32)
    for h in range(SSD_HEADS):
        expand[h, h * SSD_HEAD_DIM:(h + 1) * SSD_HEAD_DIM] = 1.0
    n_cmp = N_CMP_PAD - 1
    ci = np.arange(n_cmp)[:, None] * CMP_STRIDE
    sj = np.arange(32)[None, :] * SEL_BLOCK
    sel_map = ((ci < sj + SEL_BLOCK) & (ci + CMP_BLOCK > sj)).astype(np.float32)
    selmap_t = np.zeros((N_SLC_PAD, N_CMP_PAD), np.float32)
    selmap_t[:32, :n_cmp] = sel_map.T
    shift = np.zeros(((SSD_CONV_K - 1) * CQ, CONV_HIST + CQ), np.float32)
    for k in range(SSD_CONV_K - 1):
        for t in range(CQ):
            shift[k * CQ + t, CONV_HIST + t - (SSD_CONV_K - 1) + k] = 1.0
    return jnp.asarray(tri, BF16), jnp.asarray(expand, BF16), jnp.asarray(selmap_t, BF16), jnp.asarray(shift, BF16)


def _pad_lanes(v):
    return jnp.concatenate([v.astype(F32), jnp.zeros((LANES - v.shape[0],), F32)])[None, :]


def kernel(x, norm_w, w_in, conv_w, conv_b, dt_bias, a_log, d_skip, ssd_norm_w, q_norm_w, k_cmp_norm_w,
           k_slc_norm_w, k_win_norm_w, cmp_pe_k, cmp_w1_k, cmp_b1_k, cmp_w2_k, cmp_pe_v, cmp_w1_v, cmp_b1_v,
           cmp_w2_v, w_out_ssd, w_out_nsa, w_o):
    batch, seq, _ = x.shape
    assert seq == 2048 and (batch * seq) % IN_TM == 0
    x2 = x.reshape(batch * seq, D_MODEL)
    w_main, w_tail = _w_in_sections(w_in)
    tri, expand, selmap_t, shift = _constants()

    proj, tail = _inproj(x2, norm_w[None, :], w_main, w_tail)

    y_ssd = _ssd(proj, tail,
                 conv_w[:, :SSD_D_INNER], conv_b[None, :SSD_D_INNER],
                 conv_w[:, SSD_D_INNER:], conv_b[None, SSD_D_INNER:],
                 _pad_lanes(dt_bias), _pad_lanes(a_log),
                 jnp.repeat(d_skip, SSD_HEAD_DIM)[None, :], ssd_norm_w[None, :],
                 tri, expand, shift, batch, seq)

    kc, vc = _compress(proj, cmp_pe_k, cmp_w1_k.astype(BF16), cmp_b1_k[None, :], cmp_w2_k.astype(BF16),
                       cmp_pe_v, cmp_w1_v.astype(BF16), cmp_b1_v[None, :], cmp_w2_v.astype(BF16),
                       k_cmp_norm_w[None, :], batch, seq)

    y_nsa = _nsa(proj, tail, kc, vc, q_norm_w[None, :], k_slc_norm_w[None, :], k_win_norm_w[None, :],
                 selmap_t, batch, seq)

    merged = _merge(y_ssd, y_nsa, proj, w_out_ssd.astype(BF16), w_out_nsa.astype(BF16))
    out = _final(merged, x2, w_o.astype(BF16))
    return out.reshape(batch, seq, D_MODEL)
```

```python
import functools

import numpy as np
import jax
import jax.numpy as jnp
from jax import lax
from jax.experimental import pallas as pl
from jax.experimental.pallas import tpu as pltpu

D_MODEL = 2048
SSD_D_INNER = D_MODEL
SSD_HEAD_DIM = 64
SSD_HEADS = SSD_D_INNER // SSD_HEAD_DIM
SSD_GROUPS = 4
SSD_D_STATE = 128
SSD_CONV_K = 4
SSD_CHUNK = 128
SSD_BC_DIM = 2 * SSD_GROUPS * SSD_D_STATE
SSD_CONV_DIM = SSD_D_INNER + SSD_BC_DIM

NSA_HEADS = 16
NSA_HEAD_DIM = 128
NSA_KV_GROUPS = 2
NSA_HPG = NSA_HEADS // NSA_KV_GROUPS
NSA_D = NSA_HEADS * NSA_HEAD_DIM
NSA_KV_D = NSA_KV_GROUPS * NSA_HEAD_DIM
N_NSA_BRANCHES = 3
CMP_BLOCK = 32
CMP_STRIDE = 16
SEL_BLOCK = 64
SEL_TOPK = 8
WINDOW = 512
Q_BLOCK = 128

EPS = 1e-6
NEG_INF = -1e30
FORCED_SCORE = 1e9

LANES = 128
SUBLANES = 8
VMEM_LIMIT = 56 * 1024 * 1024

F32 = jnp.float32
BF16 = jnp.bfloat16
HIGHEST = lax.Precision.HIGHEST

C_ZSSD = 0
C_XS = 2048
C_Q = 4096
C_ZNSA = 6144
C_GLSSD = 8192
C_GLNSA = 10240
C_BC = 12288
C_KVCMP = 13312
C_KVSLC = 13824
C_KVWIN = 14336
N_MAIN = 14848
T_GATE = SSD_HEADS

_SIZES = [SSD_D_INNER, SSD_CONV_DIM, SSD_HEADS, NSA_D] + [NSA_KV_D] * 6 + [NSA_D, NSA_HEADS * N_NSA_BRANCHES, D_MODEL, D_MODEL]
_OFFS = [0] + [int(o) for o in np.cumsum(_SIZES)]


def _sigmoid(v):
    return 0.5 + 0.5 * jnp.tanh(0.5 * v)


def _silu(v):
    h = 0.5 * v
    return h + h * jnp.tanh(h)


def _softplus(v):
    return jnp.maximum(v, 0.0) + jnp.log(1.0 + jnp.exp(-jnp.abs(v)))


def _rms(v, w):
    ms = jnp.mean(v * v, axis=-1, keepdims=True)
    return v * lax.rsqrt(ms + EPS) * w


def _dot(a, b, precision=None):
    return jnp.dot(a, b, preferred_element_type=F32, precision=precision)


def _split3(v):
    hi = v.astype(BF16)
    r1 = v - hi.astype(F32)
    mid = r1.astype(BF16)
    lo = (r1 - mid.astype(F32)).astype(BF16)
    return hi, mid, lo


def _dot_nt(a, b, precision=None):
    return lax.dot_general(a, b, (((1,), (1,)), ((), ())), preferred_element_type=F32, precision=precision)


def _params(sem):
    return pltpu.CompilerParams(dimension_semantics=sem, vmem_limit_bytes=VMEM_LIMIT)


IN_TM = 1024
IN_CHUNK = 512
IN_TN = 4 * IN_CHUNK
N_PAD = -(-N_MAIN // IN_TN) * IN_TN
IN_RC = 256


def _inproj_kernel(x_ref, nw_ref, w_ref, wt_ref, o_ref, t_ref, h_ref):
    j = pl.program_id(1)

    @pl.when(j == 0)
    def _():
        def body(r, carry):
            r0 = pl.multiple_of(r * IN_RC, IN_RC)
            h_ref[pl.ds(r0, IN_RC), :] = _rms(x_ref[pl.ds(r0, IN_RC), :], nw_ref[...]).astype(BF16)
            return carry

        lax.fori_loop(0, IN_TM // IN_RC, body, 0)
        t_ref[...] = _dot(h_ref[...], wt_ref[...])

    def chunks(n_valid):
        for c in range(IN_TN // IN_CHUNK):
            cs = slice(c * IN_CHUNK, (c + 1) * IN_CHUNK)
            if c < n_valid:
                o_ref[:, cs] = _dot(h_ref[...], w_ref[:, cs]).astype(BF16)
            else:
                o_ref[:, cs] = jnp.zeros((IN_TM, IN_CHUNK), BF16)

    last = N_PAD // IN_TN - 1
    pl.when(j < last)(lambda: chunks(IN_TN // IN_CHUNK))
    pl.when(j == last)(lambda: chunks((N_MAIN - last * IN_TN) // IN_CHUNK))


def _inproj(x2, norm_w, w_main, w_tail):
    m = x2.shape[0]
    return pl.pallas_call(
        _inproj_kernel,
        grid=(m // IN_TM, N_PAD // IN_TN),
        in_specs=[
            pl.BlockSpec((IN_TM, D_MODEL), lambda i, j: (i, 0)),
            pl.BlockSpec((1, D_MODEL), lambda i, j: (0, 0)),
            pl.BlockSpec((D_MODEL, IN_TN), lambda i, j: (0, j)),
            pl.BlockSpec((D_MODEL, LANES), lambda i, j: (0, 0)),
        ],
        out_specs=[
            pl.BlockSpec((IN_TM, IN_TN), lambda i, j: (i, j)),
            pl.BlockSpec((IN_TM, LANES), lambda i, j: (i, 0)),
        ],
        out_shape=[
            jax.ShapeDtypeStruct((m, N_PAD), BF16),
            jax.ShapeDtypeStruct((m, LANES), F32),
        ],
        scratch_shapes=[pltpu.VMEM((IN_TM, D_MODEL), BF16)],
        compiler_params=_params(("arbitrary", "arbitrary")),
    )(x2, norm_w, w_main, w_tail)


CQ = SSD_CHUNK
PAIR = 2 * SSD_HEAD_DIM
N_PAIRS = SSD_HEADS // 2
PAIRS_PER_GROUP = N_PAIRS // SSD_GROUPS
CONV_HIST = 16
CONV_COLS = 256


def _ssd_kernel(z_ref, xs_ref, bc_ref, tail_ref, cwx_ref, cbx_ref, cwb_ref, cbb_ref, dtb_ref, alog_ref,
                dsk_ref, nw_ref, tri_ref, exp_ref, shift_ref, o_ref, xbuf, bbuf, xa_ref, ba_ref, st_ref, y_ref):
    c = pl.program_id(1)

    @pl.when(c == 0)
    def _():
        xbuf[0:CONV_HIST, :] = jnp.zeros((CONV_HIST, SSD_D_INNER), BF16)
        bbuf[0:CONV_HIST, :] = jnp.zeros((CONV_HIST, SSD_BC_DIM), BF16)
        st_ref[...] = jnp.zeros_like(st_ref)

    xbuf[CONV_HIST:CONV_HIST + CQ, :] = xs_ref[...]
    bbuf[CONV_HIST:CONV_HIST + CQ, :] = bc_ref[...]

    def conv(buf, cur_ref, w_ref, b_ref, out_ref):
        for j in range(buf.shape[1] // CONV_COLS):
            cs = slice(j * CONV_COLS, (j + 1) * CONV_COLS)
            shifted = _dot(shift_ref[...], buf[:, cs])
            acc = b_ref[:, cs] + cur_ref[:, cs].astype(F32) * w_ref[SSD_CONV_K - 1:SSD_CONV_K, cs]
            for k in range(SSD_CONV_K - 1):
                acc = acc + shifted[k * CQ:(k + 1) * CQ, :] * w_ref[k:k + 1, cs]
            out_ref[:, cs] = _silu(acc)

    conv(xbuf, xs_ref, cwx_ref, cbx_ref, xa_ref)
    conv(bbuf, bc_ref, cwb_ref, cbb_ref, ba_ref)
    xbuf[0:CONV_HIST, :] = xbuf[CQ:CQ + CONV_HIST, :]
    bbuf[0:CONV_HIST, :] = bbuf[CQ:CQ + CONV_HIST, :]

    dt = _softplus(tail_ref[...] + dtb_ref[...])
    la = dt * (-jnp.exp(alog_ref[...]))
    lc = sum(_dot(tri_ref[...], part) for part in _split3(la)) * LOG2E
    lc_t = lc.T
    dt_t = dt.T
    w_t = jnp.exp2(lc_t[:, CQ - 1:CQ] - lc_t) * dt_t
    cd = jnp.exp2(lc[CQ - 1:CQ, :])
    cdx = sum(_dot(part, exp_ref[...]) for part in _split3(jnp.broadcast_to(cd, (SUBLANES, LANES))))

    row = lax.broadcasted_iota(jnp.int32, (CQ, CQ), 0)
    col = lax.broadcasted_iota(jnp.int32, (CQ, CQ), 1)
    causal = row >= col
    lo_half = col < SSD_HEAD_DIM

    for g in range(SSD_GROUPS):
        b_f = ba_ref[:, g * SSD_D_STATE:(g + 1) * SSD_D_STATE]
        c_f = ba_ref[:, (SSD_GROUPS + g) * SSD_D_STATE:(SSD_GROUPS + g + 1) * SSD_D_STATE]
        cb = _dot_nt(c_f.astype(BF16), b_f.astype(BF16))
        b_t = b_f.T
        for pp in range(PAIRS_PER_GROUP):
            p = g * PAIRS_PER_GROUP + pp
            lanes = slice(p * PAIR, (p + 1) * PAIR)
            xp = xa_ref[:, lanes].astype(BF16)
            s_prev = st_ref[:, lanes]
            s_prev_b = s_prev.astype(BF16)
            x_half = (jnp.where(lo_half, xp, 0.0), jnp.where(lo_half, 0.0, xp))
            s_half = (jnp.where(lo_half, s_prev_b, 0.0), jnp.where(lo_half, 0.0, s_prev_b))
            y_pair, s_new = None, None
            for hh in range(2):
                h = 2 * p + hh
                lcol = jnp.broadcast_to(lc[:, h:h + 1], (CQ, CQ))
                lmat = jnp.exp2(jnp.where(causal, lcol - lc_t[h:h + 1, :], -jnp.inf))
                m_h = (cb * lmat * dt_t[h:h + 1, :]).astype(BF16)
                c_s = (c_f * jnp.exp2(lcol)).astype(BF16)
                y_h = _dot(m_h, x_half[hh]) + _dot(c_s, s_half[hh])
                w_h = (b_t * w_t[h:h + 1, :]).astype(BF16)
                s_h = _dot(w_h, x_half[hh])
                y_pair = y_h if y_pair is None else y_pair + y_h
                s_new = s_h if s_new is None else s_new + s_h
            y_ref[:, lanes] = y_pair
            st_ref[:, lanes] = s_prev * cdx[0:1, lanes] + s_new

    y = y_ref[...] + xa_ref[...] * dsk_ref[...]
    y = y * _silu(z_ref[...].astype(F32))
    o_ref[...] = _rms(y, nw_ref[...]).astype(BF16)


def _ssd(proj, tail, cwx, cbx, cwb, cbb, dtb, alog, dsk, nw, tri, expand, shift, batch, seq):
    nc = seq // CQ
    m = batch * seq

    def full(shape):
        return pl.BlockSpec(shape, lambda b, c: (0,) * len(shape))

    return pl.pallas_call(
        _ssd_kernel,
        grid=(batch, nc),
        in_specs=[
            pl.BlockSpec((CQ, SSD_D_INNER), lambda b, c: (b * nc + c, C_ZSSD // SSD_D_INNER)),
            pl.BlockSpec((CQ, SSD_D_INNER), lambda b, c: (b * nc + c, C_XS // SSD_D_INNER)),
            pl.BlockSpec((CQ, SSD_BC_DIM), lambda b, c: (b * nc + c, C_BC // SSD_BC_DIM)),
            pl.BlockSpec((CQ, LANES), lambda b, c: (b * nc + c, 0)),
            full((SSD_CONV_K, SSD_D_INNER)), full((1, SSD_D_INNER)),
            full((SSD_CONV_K, SSD_BC_DIM)), full((1, SSD_BC_DIM)),
            full((1, LANES)), full((1, LANES)),
            full((1, SSD_D_INNER)), full((1, SSD_D_INNER)),
            full((CQ, CQ)), full((LANES, SSD_D_INNER)), full(((SSD_CONV_K - 1) * CQ, CONV_HIST + CQ)),
        ],
        out_specs=pl.BlockSpec((CQ, SSD_D_INNER), lambda b, c: (b * nc + c, 0)),
        out_shape=jax.ShapeDtypeStruct((m, SSD_D_INNER), BF16),
        scratch_shapes=[
            pltpu.VMEM((CONV_HIST + CQ, SSD_D_INNER), BF16),
            pltpu.VMEM((CONV_HIST + CQ, SSD_BC_DIM), BF16),
            pltpu.VMEM((CQ, SSD_D_INNER), F32),
            pltpu.VMEM((CQ, SSD_BC_DIM), F32),
            pltpu.VMEM((SSD_D_STATE, SSD_D_INNER), F32),
            pltpu.VMEM((CQ, SSD_D_INNER), F32),
        ],
        compiler_params=_params(("arbitrary", "arbitrary")),
    )(proj, proj, proj, tail, cwx, cbx, cwb, cbb, dtb, alog, dsk, nw, tri, expand, shift)


N_CMP_PAD = 128
CMP_HALF = CMP_BLOCK // CMP_STRIDE


def _cmp_kernel(kv_ref, pek_ref, w1k_ref, b1k_ref, w2k_ref, pev_ref, w1v_ref, b1v_ref, w2v_ref, knw_ref,
                kc_ref, vc_ref, buf):
    hd = NSA_HEAD_DIM
    for slab in range(2 * NSA_KV_GROUPS):
        buf[slab] = kv_ref[:, slab * hd:(slab + 1) * hd].astype(F32)
    for kind, (pe_ref, w1_ref, b1_ref, w2_ref, out_ref) in enumerate((
            (pek_ref, w1k_ref, b1k_ref, w2k_ref, kc_ref), (pev_ref, w1v_ref, b1v_ref, w2v_ref, vc_ref))):
        for g in range(NSA_KV_GROUPS):
            slab = kind * NSA_KV_GROUPS + g
            u = [jnp.zeros((N_CMP_PAD, hd), F32) for _ in range(CMP_HALF)]
            for r in range(CMP_STRIDE):
                xr = buf[slab, pl.ds(r, N_CMP_PAD, stride=CMP_STRIDE), :]
                for a in range(CMP_HALF):
                    l = a * CMP_STRIDE + r
                    u[a] = u[a] + _dot((xr + pe_ref[l:l + 1, :]).astype(BF16), w1_ref[l * hd:(l + 1) * hd, :])
            acc = u[0] + pltpu.roll(u[1], N_CMP_PAD - 1, 0) + b1_ref[...]
            out = _dot(_silu(acc).astype(BF16), w2_ref[...])
            if kind == 0:
                out_ref[0, :, g * hd:(g + 1) * hd] = _rms(out, knw_ref[...]).astype(BF16)
            else:
                out_ref[0, g * hd:(g + 1) * hd, :] = out.T.astype(BF16)


def _compress(proj, pek, w1k, b1k, w2k, pev, w1v, b1v, w2v, knw, batch, seq):
    hd = NSA_HEAD_DIM

    def full(shape):
        return pl.BlockSpec(shape, lambda b: (0,) * len(shape))

    return pl.pallas_call(
        _cmp_kernel,
        grid=(batch,),
        in_specs=[
            pl.BlockSpec((seq, 2 * NSA_KV_D), lambda b: (b, C_KVCMP // (2 * NSA_KV_D))),
            full((CMP_BLOCK, hd)), full((CMP_BLOCK * hd, hd)), full((1, hd)), full((hd, hd)),
            full((CMP_BLOCK, hd)), full((CMP_BLOCK * hd, hd)), full((1, hd)), full((hd, hd)),
            full((1, hd)),
        ],
        out_specs=[pl.BlockSpec((1, N_CMP_PAD, NSA_KV_D), lambda b: (b, 0, 0)),
                   pl.BlockSpec((1, NSA_KV_D, N_CMP_PAD), lambda b: (b, 0, 0))],
        out_shape=[jax.ShapeDtypeStruct((batch, N_CMP_PAD, NSA_KV_D), BF16),
                   jax.ShapeDtypeStruct((batch, NSA_KV_D, N_CMP_PAD), BF16)],
        scratch_shapes=[pltpu.VMEM((2 * NSA_KV_GROUPS, seq, hd), F32)],
        compiler_params=_params(("arbitrary",)),
    )(proj, pek, w1k, b1k, w2k, pev, w1v, b1v, w2v, knw)


QB = Q_BLOCK
GROWS = NSA_HPG * QB
TK = 256
BR_SEL, BR_WIN = 0, 1
N_SLC = 32
N_SLC_PAD = 128
NORM_RC = 256
KT = 128
PW = 2 * QB
N_HPAIRS = NSA_HPG // 2
KX = 2 * NSA_HEAD_DIM
VT_ROWS = NSA_HEAD_DIM + 16
LOG2E = 1.4426950408889634


def _nsa_kernel(q_ref, kvs_ref, kvw_ref, kc_ref, vct_ref, z_ref, tail_ref, qnw_ref, ksnw_ref,
                kwnw_ref, selmap_ref, o_ref, kx_s, vt_s, qt_s, oc_s, gt_s, m_s, acc_s, s_s, p_s):
    i = pl.program_id(1)
    qs = i * QB
    hd = NSA_HEAD_DIM
    seq = kvs_ref.shape[0]

    def prepare(q_r, tail_r, q0):
        qw = qnw_ref[...] * (hd ** -0.5 * LOG2E)
        for h in range(NSA_HEADS):
            qh = q_r[:, h * hd:(h + 1) * hd].astype(F32)
            qt_s[0:hd, h * QB:(h + 1) * QB] = _rms(qh, qw).T.astype(BF16)
        gt_s[...] = _sigmoid(tail_r[...]).T

        c_idx = lax.broadcasted_iota(jnp.int32, (N_CMP_PAD, GROWS), 0)
        q_in = lax.broadcasted_iota(jnp.int32, (N_CMP_PAD, GROWS), 1) & (QB - 1)
        mask_c = c_idx * CMP_STRIDE + (CMP_BLOCK - 1) <= q0 + q_in
        any_c = CMP_BLOCK - 1 <= q0 + q_in[0:1, :]
        jj = lax.broadcasted_iota(jnp.int32, (N_SLC, QB), 0)
        tq_t = q0 + lax.broadcasted_iota(jnp.int32, (N_SLC, QB), 1)

        for g in range(NSA_KV_GROUPS):
            gl = slice(g * hd, (g + 1) * hd)
            gcols = slice(g * GROWS, (g + 1) * GROWS)
            s = _dot(kc_ref[0, :, gl], qt_s[0:hd, gcols])
            s = jnp.where(mask_c, s, NEG_INF)
            e = jnp.exp2(s - jnp.max(s, axis=0, keepdims=True))
            p_cmp = e * jnp.where(any_c, 1.0 / jnp.sum(e, axis=0, keepdims=True), 0.0)
            oc_s[:, gcols] = _dot(vct_ref[0, gl, :], p_cmp.astype(BF16))

            p_sum = p_cmp[:, 0:QB]
            for k in range(1, NSA_HPG):
                p_sum = p_sum + p_cmp[:, k * QB:(k + 1) * QB]
            imp = sum(_dot(selmap_ref[...], part) for part in _split3(p_sum))[0:N_SLC, :]
            forced = (jj == (tq_t >> 6)) | (jj == 0)
            imp = jnp.where(forced, FORCED_SCORE, jnp.where(jj * SEL_BLOCK <= tq_t, imp, -1.0))
            rows = [slice(r, r + SUBLANES) for r in range(0, N_SLC, SUBLANES)]
            ranks = [jnp.zeros((SUBLANES, QB), jnp.int32) for _ in rows]
            for j2 in range(N_SLC):
                other = imp[j2:j2 + 1, :]
                for r, rs in enumerate(rows):
                    if rs.stop <= j2:
                        beats = other > imp[rs, :]
                    elif rs.start > j2:
                        beats = other >= imp[rs, :]
                    else:
                        beats = (other > imp[rs, :]) | ((other == imp[rs, :]) & (jj[rs, :] > j2))
                    ranks[r] = ranks[r] + jnp.where(beats, 1, 0)
            rank = jnp.concatenate(ranks, axis=0)
            picked = (rank < SEL_TOPK) & (jj * SEL_BLOCK <= tq_t)
            sel_bias = jnp.concatenate([jnp.where(picked, 0.0, NEG_INF).astype(F32),
                                        jnp.zeros((hd - N_SLC, QB), F32)], axis=0).astype(BF16)
            for k in range(NSA_HPG):
                qt_s[hd:KX, (g * NSA_HPG + k) * QB:(g * NSA_HPG + k + 1) * QB] = sel_bias

    @pl.when(i == 0)
    def _():
        for br, src, nw in ((BR_SEL, kvs_ref, ksnw_ref), (BR_WIN, kvw_ref, kwnw_ref)):
            for g in range(NSA_KV_GROUPS):
                def body(r, carry, br=br, src=src, nw=nw, g=g):
                    r0 = pl.multiple_of(r * NORM_RC, NORM_RC)
                    kk = src[pl.ds(r0, NORM_RC), g * hd:(g + 1) * hd].astype(F32)
                    kx_s[br, pl.ds(r0, NORM_RC), g * KX:g * KX + hd] = _rms(kk, nw[...]).astype(BF16)
                    blk = (r0 + lax.broadcasted_iota(jnp.int32, (NORM_RC, hd), 0)) >> 6
                    hit = blk == lax.broadcasted_iota(jnp.int32, (NORM_RC, hd), 1)
                    aux = jnp.where(hit, 1.0 if br == BR_SEL else 0.0, 0.0)
                    kx_s[br, pl.ds(r0, NORM_RC), g * KX + hd:(g + 1) * KX] = aux.astype(BF16)
                    return carry

                lax.fori_loop(0, seq // NORM_RC, body, 0)

                def body_t(r, carry, br=br, src=src, g=g):
                    r0 = pl.multiple_of(r * KT, KT)
                    vv = src[pl.ds(r0, KT), NSA_KV_D + g * hd:NSA_KV_D + (g + 1) * hd].astype(F32)
                    vt_s[br, r, g, 0:hd, :] = vv.T.astype(BF16)
                    vt_s[br, r, g, hd:VT_ROWS, :] = jnp.ones((VT_ROWS - hd, KT), BF16)
                    return carry

                lax.fori_loop(0, seq // KT, body_t, 0)

    prepare(q_ref, tail_ref, qs)

    n_sel = (qs + QB + TK - 1) // TK
    win_first = jnp.maximum(qs - (WINDOW - 1), 0) // TK
    n_win_mid = jnp.maximum(n_sel - win_first - 2, 0)
    n_plain = n_sel - 1 + n_win_mid
    n_tiles = n_plain + 2 + jnp.where(n_sel - win_first >= 2, 1, 0)

    def tile_of(u):
        e = u - n_plain
        br = jnp.where(u < n_plain, jnp.where(u < n_sel - 1, BR_SEL, BR_WIN), jnp.where(e == 0, BR_SEL, BR_WIN))
        t = jnp.where(u < n_plain, jnp.where(u < n_sel - 1, u, win_first + 1 + (u - (n_sel - 1))),
                      jnp.where(e == 1, win_first, n_sel - 1))
        return br, t

    rel0 = (lax.broadcasted_iota(jnp.int32, (TK, QB), 0) - lax.broadcasted_iota(jnp.int32, (TK, QB), 1)) - qs

    m_s[...] = jnp.full_like(m_s, NEG_INF)
    acc_s[...] = jnp.zeros_like(acc_s)
    units = [(g, slice((g * N_HPAIRS + pr) * PW, (g * N_HPAIRS + pr + 1) * PW))
             for g in range(NSA_KV_GROUPS) for pr in range(N_HPAIRS)]

    def keys_of(br, t):
        k0 = pl.multiple_of(t * TK, TK)
        return [kx_s[br, pl.ds(k0, TK), g * KX:(g + 1) * KX] for g in range(NSA_KV_GROUPS)]

    def values_of(br, t):
        return [jnp.concatenate([vt_s[br, t * (TK // KT) + u, g] for u in range(TK // KT)], axis=1)
                for g in range(NSA_KV_GROUPS)]

    def finish_unit(br, v_tt, g, cols, alpha):
        acc_s[br, :, cols] = alpha * acc_s[br, :, cols] + _dot(v_tt[g], p_s[:, cols])

    def finish(br, t, alphas):
        v_tt = values_of(br, t)
        for (g, cols), alpha in zip(units, alphas):
            finish_unit(br, v_tt, g, cols, alpha)

    def tile_step(u, carry, edge, prefetch=True):
        alphas_prev, br_prev, t_prev = carry
        br, t = tile_of(u)
        if prefetch:
            k_next = keys_of(*tile_of(u + 1))
        v_prev = values_of(br_prev, t_prev)
        if edge:
            rel = rel0 + t * TK
            ok = (rel <= 0) & (rel > jnp.where(br == BR_WIN, -WINDOW, -(1 << 30)))
            bias = jnp.where(ok, 0.0, NEG_INF)
            bias = jnp.concatenate([bias, bias], axis=1)
        alphas = []
        for (g, cols), alpha_prev in zip(units, alphas_prev):
            s2 = s_s[:, cols]
            if edge:
                s2 = s2 + bias
            if prefetch:
                s_s[:, cols] = _dot(k_next[g], qt_s[:, cols])
            finish_unit(br_prev, v_prev, g, cols, alpha_prev)
            m_prev = m_s[br, :, cols]
            m_new = jnp.maximum(m_prev, jnp.max(s2, axis=0, keepdims=True))
            m_s[br, :, cols] = m_new
            p_s[:, cols] = jnp.exp2(s2 - m_new).astype(BF16)
            alphas.append(jnp.exp2(m_prev - m_new))
        return tuple(alphas), br, t

    k_first = keys_of(*tile_of(0))
    for g, cols in units:
        s_s[:, cols] = _dot(k_first[g], qt_s[:, cols])
    p_s[...] = jnp.zeros_like(p_s)
    carry = (tuple(jnp.ones((1, PW), F32) for _ in units), jnp.int32(BR_SEL), jnp.int32(0))
    carry = lax.fori_loop(0, n_plain, lambda u, c: tile_step(u, c, edge=False), carry)
    carry = lax.fori_loop(n_plain, n_tiles - 1, lambda u, c: tile_step(u, c, edge=True), carry)
    alphas_last, br_last, t_last = tile_step(n_tiles - 1, carry, edge=True, prefetch=False)
    finish(br_last, t_last, alphas_last)

    for g, cols in units:
        heads = [cols.start // QB + u for u in range(2)]

        def gate_row(branch):
            return jnp.concatenate([gt_s[T_GATE + h * N_NSA_BRANCHES + branch:
                                         T_GATE + h * N_NSA_BRANCHES + branch + 1, :] for h in heads], axis=1)

        o_t = (gate_row(0) * oc_s[:, cols]
               + (gate_row(1) / acc_s[BR_SEL, hd:hd + 1, cols]) * acc_s[BR_SEL, 0:hd, cols]
               + (gate_row(2) / acc_s[BR_WIN, hd:hd + 1, cols]) * acc_s[BR_WIN, 0:hd, cols])
        for u, h in enumerate(heads):
            zz = z_ref[:, h * hd:(h + 1) * hd].astype(F32)
            o_ref[:, h * hd:(h + 1) * hd] = (o_t[:, u * QB:(u + 1) * QB].T * _silu(zz)).astype(BF16)


def _nsa(proj, tail, kc, vc, qnw, ksnw, kwnw, selmap_t, batch, seq):
    nq = seq // QB
    hd = NSA_HEAD_DIM
    m = batch * seq

    def full(shape):
        return pl.BlockSpec(shape, lambda b, i: (0,) * len(shape))

    return pl.pallas_call(
        _nsa_kernel,
        grid=(batch, nq),
        in_specs=[
            pl.BlockSpec((QB, NSA_D), lambda b, i: (b * nq + i, C_Q // NSA_D)),
            pl.BlockSpec((seq, 2 * NSA_KV_D), lambda b, i: (b, C_KVSLC // (2 * NSA_KV_D))),
            pl.BlockSpec((seq, 2 * NSA_KV_D), lambda b, i: (b, C_KVWIN // (2 * NSA_KV_D))),
            pl.BlockSpec((1, N_CMP_PAD, NSA_KV_D), lambda b, i: (b, 0, 0)),
            pl.BlockSpec((1, NSA_KV_D, N_CMP_PAD), lambda b, i: (b, 0, 0)),
            pl.BlockSpec((QB, NSA_D), lambda b, i: (b * nq + i, C_ZNSA // NSA_D)),
            pl.BlockSpec((QB, LANES), lambda b, i: (b * nq + i, 0)),
            full((1, hd)), full((1, hd)), full((1, hd)),
            full((N_SLC_PAD, N_CMP_PAD)),
        ],
        out_specs=pl.BlockSpec((QB, NSA_D), lambda b, i: (b * nq + i, 0)),
        out_shape=jax.ShapeDtypeStruct((m, NSA_D), BF16),
        scratch_shapes=[
            pltpu.VMEM((2, seq, NSA_KV_GROUPS * KX), BF16),
            pltpu.VMEM((2, seq // KT, NSA_KV_GROUPS, VT_ROWS, KT), BF16),
            pltpu.VMEM((KX, NSA_HEADS * QB), BF16),
            pltpu.VMEM((hd, NSA_HEADS * QB), F32),
            pltpu.VMEM((LANES, QB), F32),
            pltpu.VMEM((2, 1, NSA_HEADS * QB), F32),
            pltpu.VMEM((2, VT_ROWS, NSA_HEADS * QB), F32),
            pltpu.VMEM((TK, NSA_HEADS * QB), F32),
            pltpu.VMEM((TK, NSA_HEADS * QB), BF16),
        ],
        compiler_params=_params(("arbitrary", "arbitrary")),
    )(proj, proj, proj, kc, vc, proj, tail, qnw, ksnw, kwnw, selmap_t)


OUT_TM = 1024
OUT_TN = 512
MRG_TN = 2 * OUT_TN


def _merge_kernel(ys_ref, yn_ref, gs_ref, gn_ref, ws_ref, wn_ref, o_ref):
    for c in range(MRG_TN // OUT_TN):
        cs = slice(c * OUT_TN, (c + 1) * OUT_TN)
        a = _dot(ys_ref[...], ws_ref[:, cs])
        b = _dot(yn_ref[...], wn_ref[:, cs])
        merged = _sigmoid(gs_ref[:, cs].astype(F32)) * a + _sigmoid(gn_ref[:, cs].astype(F32)) * b
        o_ref[:, cs] = merged.astype(BF16)


def _merge(y_ssd, y_nsa, proj, w_s, w_n):
    m = y_ssd.shape[0]
    nb = D_MODEL // MRG_TN
    return pl.pallas_call(
        _merge_kernel,
        grid=(nb, m // OUT_TM),
        in_specs=[
            pl.BlockSpec((OUT_TM, D_MODEL), lambda j, i: (i, 0)),
            pl.BlockSpec((OUT_TM, D_MODEL), lambda j, i: (i, 0)),
            pl.BlockSpec((OUT_TM, MRG_TN), lambda j, i: (i, C_GLSSD // MRG_TN + j)),
            pl.BlockSpec((OUT_TM, MRG_TN), lambda j, i: (i, C_GLNSA // MRG_TN + j)),
            pl.BlockSpec((D_MODEL, MRG_TN), lambda j, i: (0, j)),
            pl.BlockSpec((D_MODEL, MRG_TN), lambda j, i: (0, j)),
        ],
        out_specs=pl.BlockSpec((OUT_TM, MRG_TN), lambda j, i: (i, j)),
        out_shape=jax.ShapeDtypeStruct((m, D_MODEL), BF16),
        compiler_params=_params(("arbitrary", "arbitrary")),
    )(y_ssd, y_nsa, proj, proj, w_s, w_n)


FIN_TM = 512


def _final_kernel(m_ref, x_ref, w_ref, o_ref):
    for j in range(D_MODEL // OUT_TN):
        cs = slice(j * OUT_TN, (j + 1) * OUT_TN)
        o_ref[:, cs] = x_ref[:, cs] + _dot(m_ref[...], w_ref[:, cs])


def _final(merged, x2, w_o):
    m = merged.shape[0]
    return pl.pallas_call(
        _final_kernel,
        grid=(m // FIN_TM,),
        in_specs=[
            pl.BlockSpec((FIN_TM, D_MODEL), lambda i: (i, 0)),
            pl.BlockSpec((FIN_TM, D_MODEL), lambda i: (i, 0)),
            pl.BlockSpec((D_MODEL, D_MODEL), lambda i: (0, 0)),
        ],
        out_specs=pl.BlockSpec((FIN_TM, D_MODEL), lambda i: (i, 0)),
        out_shape=jax.ShapeDtypeStruct((m, D_MODEL), F32),
        compiler_params=_params(("arbitrary",)),
    )(merged, x2, w_o)


def _w_in_sections(w_in):
    sec = {name: w_in[:, _OFFS[i]:_OFFS[i + 1]] for i, name in enumerate(
        ("z_ssd", "xbc", "dt", "q", "k_cmp", "v_cmp", "k_slc", "v_slc", "k_win", "v_win",
         "z_nsa", "gates", "gl_ssd", "gl_nsa"))}
    main = jnp.concatenate([
        sec["z_ssd"], sec["xbc"][:, :SSD_D_INNER], sec["q"], sec["z_nsa"], sec["gl_ssd"], sec["gl_nsa"],
        sec["xbc"][:, SSD_D_INNER:], sec["k_cmp"], sec["v_cmp"], sec["k_slc"], sec["v_slc"],
        sec["k_win"], sec["v_win"], jnp.zeros((D_MODEL, N_PAD - N_MAIN), w_in.dtype)], axis=1).astype(BF16)
    pad = LANES - SSD_HEADS - NSA_HEADS * N_NSA_BRANCHES
    tail = jnp.concatenate([sec["dt"], sec["gates"], jnp.zeros((D_MODEL, pad), w_in.dtype)], axis=1).astype(BF16)
    return main, tail


def _constants():
    tri = np.tril(np.ones((CQ, CQ), np.float32))
    expand = np.zeros((LANES, SSD_D_INNER), np.float32)
    for h in range(SSD_HEADS):
        expand[h, h * SSD_HEAD_DIM:(h + 1) * SSD_HEAD_DIM] = 1.0
    n_cmp = N_CMP_PAD - 1
    ci = np.arange(n_cmp)[:, None] * CMP_STRIDE
    sj = np.arange(32)[None, :] * SEL_BLOCK
    sel_map = ((ci < sj + SEL_BLOCK) & (ci + CMP_BLOCK > sj)).astype(np.float32)
    selmap_t = np.zeros((N_SLC_PAD, N_CMP_PAD), np.float32)
    selmap_t[:32, :n_cmp] = sel_map.T
    shift = np.zeros(((SSD_CONV_K - 1) * CQ, CONV_HIST + CQ), np.float32)
    for k in range(SSD_CONV_K - 1):
        for t in range(CQ):
            shift[k * CQ + t, CONV_HIST + t - (SSD_CONV_K - 1) + k] = 1.0
    return jnp.asarray(tri, BF16), jnp.asarray(expand, BF16), jnp.asarray(selmap_t, BF16), jnp.asarray(shift, BF16)


def _pad_lanes(v):
    return jnp.concatenate([v.astype(F32), jnp.zeros((LANES - v.shape[0],), F32)])[None, :]


def kernel(x, norm_w, w_in, conv_w, conv_b, dt_bias, a_log, d_skip, ssd_norm_w, q_norm_w, k_cmp_norm_w,
           k_slc_norm_w, k_win_norm_w, cmp_pe_k, cmp_w1_k, cmp_b1_k, cmp_w2_k, cmp_pe_v, cmp_w1_v, cmp_b1_v,
           cmp_w2_v, w_out_ssd, w_out_nsa, w_o):
    batch, seq, _ = x.shape
    assert seq == 2048 and (batch * seq) % IN_TM == 0
    x2 = x.reshape(batch * seq, D_MODEL)
    w_main, w_tail = _w_in_sections(w_in)
    tri, expand, selmap_t, shift = _constants()

    proj, tail = _inproj(x2, norm_w[None, :], w_main, w_tail)

    y_ssd = _ssd(proj, tail,
                 conv_w[:, :SSD_D_INNER], conv_b[None, :SSD_D_INNER],
                 conv_w[:, SSD_D_INNER:], conv_b[None, SSD_D_INNER:],
                 _pad_lanes(dt_bias), _pad_lanes(a_log),
                 jnp.repeat(d_skip, SSD_HEAD_DIM)[None, :], ssd_norm_w[None, :],
                 tri, expand, shift, batch, seq)

    kc, vc = _compress(proj, cmp_pe_k, cmp_w1_k.astype(BF16), cmp_b1_k[None, :], cmp_w2_k.astype(BF16),
                       cmp_pe_v, cmp_w1_v.astype(BF16), cmp_b1_v[None, :], cmp_w2_v.astype(BF16),
                       k_cmp_norm_w[None, :], batch, seq)

    y_nsa = _nsa(proj, tail, kc, vc, q_norm_w[None, :], k_slc_norm_w[None, :], k_win_norm_w[None, :],
                 selmap_t, batch, seq)

    merged = _merge(y_ssd, y_nsa, proj, w_out_ssd.astype(BF16), w_out_nsa.astype(BF16))
    out = _final(merged, x2, w_o.astype(BF16))
    return out.reshape(batch, seq, D_MODEL)
```

```python
import functools

import numpy as np
import jax
import jax.numpy as jnp
from jax import lax
from jax.experimental import pallas as pl
from jax.experimental.pallas import tpu as pltpu

D_MODEL = 2048
SSD_D_INNER = D_MODEL
SSD_HEAD_DIM = 64
SSD_HEADS = SSD_D_INNER // SSD_HEAD_DIM
SSD_GROUPS = 4
SSD_D_STATE = 128
SSD_CONV_K = 4
SSD_CHUNK = 128
SSD_BC_DIM = 2 * SSD_GROUPS * SSD_D_STATE
SSD_CONV_DIM = SSD_D_INNER + SSD_BC_DIM

NSA_HEADS = 16
NSA_HEAD_DIM = 128
NSA_KV_GROUPS = 2
NSA_HPG = NSA_HEADS // NSA_KV_GROUPS
NSA_D = NSA_HEADS * NSA_HEAD_DIM
NSA_KV_D = NSA_KV_GROUPS * NSA_HEAD_DIM
N_NSA_BRANCHES = 3
CMP_BLOCK = 32
CMP_STRIDE = 16
SEL_BLOCK = 64
SEL_TOPK = 8
WINDOW = 512
Q_BLOCK = 128

EPS = 1e-6
NEG_INF = -1e30
FORCED_SCORE = 1e9

LANES = 128
SUBLANES = 8
VMEM_LIMIT = 56 * 1024 * 1024

F32 = jnp.float32
BF16 = jnp.bfloat16
HIGHEST = lax.Precision.HIGHEST

C_ZSSD = 0
C_XS = 2048
C_Q = 4096
C_ZNSA = 6144
C_GLSSD = 8192
C_GLNSA = 10240
C_BC = 12288
C_KVCMP = 13312
C_KVSLC = 13824
C_KVWIN = 14336
N_MAIN = 14848
T_GATE = SSD_HEADS

_SIZES = [SSD_D_INNER, SSD_CONV_DIM, SSD_HEADS, NSA_D] + [NSA_KV_D] * 6 + [NSA_D, NSA_HEADS * N_NSA_BRANCHES, D_MODEL, D_MODEL]
_OFFS = [0] + [int(o) for o in np.cumsum(_SIZES)]


def _sigmoid(v):
    return 0.5 + 0.5 * jnp.tanh(0.5 * v)


def _silu(v):
    h = 0.5 * v
    return h + h * jnp.tanh(h)


def _softplus(v):
    return jnp.maximum(v, 0.0) + jnp.log(1.0 + jnp.exp(-jnp.abs(v)))


def _rms(v, w):
    ms = jnp.mean(v * v, axis=-1, keepdims=True)
    return v * lax.rsqrt(ms + EPS) * w


def _dot(a, b, precision=None):
    return jnp.dot(a, b, preferred_element_type=F32, precision=precision)


def _split3(v):
    hi = v.astype(BF16)
    r1 = v - hi.astype(F32)
    mid = r1.astype(BF16)
    lo = (r1 - mid.astype(F32)).astype(BF16)
    return hi, mid, lo


def _dot_nt(a, b, precision=None):
    return lax.dot_general(a, b, (((1,), (1,)), ((), ())), preferred_element_type=F32, precision=precision)


def _params(sem):
    return pltpu.CompilerParams(dimension_semantics=sem, vmem_limit_bytes=VMEM_LIMIT)


IN_TM = 1024
IN_CHUNK = 512
IN_TN = 4 * IN_CHUNK
N_PAD = -(-N_MAIN // IN_TN) * IN_TN
IN_RC = 256


def _inproj_kernel(x_ref, nw_ref, w_ref, wt_ref, o_ref, t_ref, h_ref):
    j = pl.program_id(1)

    @pl.when(j == 0)
    def _():
        def body(r, carry):
            r0 = pl.multiple_of(r * IN_RC, IN_RC)
            h_ref[pl.ds(r0, IN_RC), :] = _rms(x_ref[pl.ds(r0, IN_RC), :], nw_ref[...]).astype(BF16)
            return carry

        lax.fori_loop(0, IN_TM // IN_RC, body, 0)
        t_ref[...] = _dot(h_ref[...], wt_ref[...])

    def chunks(n_valid):
        for c in range(IN_TN // IN_CHUNK):
            cs = slice(c * IN_CHUNK, (c + 1) * IN_CHUNK)
            if c < n_valid:
                o_ref[:, cs] = _dot(h_ref[...], w_ref[:, cs]).astype(BF16)
            else:
                o_ref[:, cs] = jnp.zeros((IN_TM, IN_CHUNK), BF16)

    last = N_PAD // IN_TN - 1
    pl.when(j < last)(lambda: chunks(IN_TN // IN_CHUNK))
    pl.when(j == last)(lambda: chunks((N_MAIN - last * IN_TN) // IN_CHUNK))


def _inproj(x2, norm_w, w_main, w_tail):
    m = x2.shape[0]
    return pl.pallas_call(
        _inproj_kernel,
        grid=(m // IN_TM, N_PAD // IN_TN),
        in_specs=[
            pl.BlockSpec((IN_TM, D_MODEL), lambda i, j: (i, 0)),
            pl.BlockSpec((1, D_MODEL), lambda i, j: (0, 0)),
            pl.BlockSpec((D_MODEL, IN_TN), lambda i, j: (0, j)),
            pl.BlockSpec((D_MODEL, LANES), lambda i, j: (0, 0)),
        ],
        out_specs=[
            pl.BlockSpec((IN_TM, IN_TN), lambda i, j: (i, j)),
            pl.BlockSpec((IN_TM, LANES), lambda i, j: (i, 0)),
        ],
        out_shape=[
            jax.ShapeDtypeStruct((m, N_PAD), BF16),
            jax.ShapeDtypeStruct((m, LANES), F32),
        ],
        scratch_shapes=[pltpu.VMEM((IN_TM, D_MODEL), BF16)],
        compiler_params=_params(("arbitrary", "arbitrary")),
    )(x2, norm_w, w_main, w_tail)


CQ = SSD_CHUNK
PAIR = 2 * SSD_HEAD_DIM
N_PAIRS = SSD_HEADS // 2
PAIRS_PER_GROUP = N_PAIRS // SSD_GROUPS
SSD_SUB = 4
CONV_HIST = 16
CONV_COLS = 256


def _ssd_kernel(*refs):
    xbuf, bbuf, st_ref = refs[16], refs[17], refs[20]

    @pl.when(pl.program_id(1) == 0)
    def _():
        xbuf[0:CONV_HIST, :] = jnp.zeros((CONV_HIST, SSD_D_INNER), BF16)
        bbuf[0:CONV_HIST, :] = jnp.zeros((CONV_HIST, SSD_BC_DIM), BF16)
        st_ref[...] = jnp.zeros_like(st_ref)

    def body(sub, carry):
        _ssd_chunk(pl.ds(pl.multiple_of(sub * CQ, CQ), CQ), *refs)
        return carry

    lax.fori_loop(0, SSD_SUB, body, 0)


def _ssd_chunk(rows, z_ref, xs_ref, bc_ref, tail_ref, cwx_ref, cbx_ref, cwb_ref, cbb_ref, dtb_ref, alog_ref,
               dsk_ref, nw_ref, tri_ref, exp_ref, shift_ref, o_ref, xbuf, bbuf, xa_ref, ba_ref, st_ref, y_ref):
    xbuf[CONV_HIST:CONV_HIST + CQ, :] = xs_ref[rows, :]
    bbuf[CONV_HIST:CONV_HIST + CQ, :] = bc_ref[rows, :]

    def conv(buf, cur_ref, w_ref, b_ref, out_ref):
        for j in range(buf.shape[1] // CONV_COLS):
            cs = slice(j * CONV_COLS, (j + 1) * CONV_COLS)
            shifted = _dot(shift_ref[...], buf[:, cs])
            acc = b_ref[:, cs] + cur_ref[rows, cs].astype(F32) * w_ref[SSD_CONV_K - 1:SSD_CONV_K, cs]
            for k in range(SSD_CONV_K - 1):
                acc = acc + shifted[k * CQ:(k + 1) * CQ, :] * w_ref[k:k + 1, cs]
            out_ref[:, cs] = _silu(acc)

    conv(xbuf, xs_ref, cwx_ref, cbx_ref, xa_ref)
    conv(bbuf, bc_ref, cwb_ref, cbb_ref, ba_ref)
    xbuf[0:CONV_HIST, :] = xbuf[CQ:CQ + CONV_HIST, :]
    bbuf[0:CONV_HIST, :] = bbuf[CQ:CQ + CONV_HIST, :]

    dt = _softplus(tail_ref[rows, :] + dtb_ref[...])
    la = dt * (-jnp.exp(alog_ref[...]))
    lc = sum(_dot(tri_ref[...], part) for part in _split3(la)) * LOG2E
    lc_t = lc.T
    dt_t = dt.T
    w_t = jnp.exp2(lc_t[:, CQ - 1:CQ] - lc_t) * dt_t
    cd = jnp.exp2(lc[CQ - 1:CQ, :])
    cdx = sum(_dot(part, exp_ref[...]) for part in _split3(jnp.broadcast_to(cd, (SUBLANES, LANES))))

    row = lax.broadcasted_iota(jnp.int32, (CQ, CQ), 0)
    col = lax.broadcasted_iota(jnp.int32, (CQ, CQ), 1)
    causal = row >= col
    lo_half = col < SSD_HEAD_DIM

    for g in range(SSD_GROUPS):
        b_f = ba_ref[:, g * SSD_D_STATE:(g + 1) * SSD_D_STATE]
        c_f = ba_ref[:, (SSD_GROUPS + g) * SSD_D_STATE:(SSD_GROUPS + g + 1) * SSD_D_STATE]
        cb = _dot_nt(c_f.astype(BF16), b_f.astype(BF16))
        b_t = b_f.T
        for pp in range(PAIRS_PER_GROUP):
            p = g * PAIRS_PER_GROUP + pp
            lanes = slice(p * PAIR, (p + 1) * PAIR)
            xp = xa_ref[:, lanes].astype(BF16)
            s_prev = st_ref[:, lanes]
            s_prev_b = s_prev.astype(BF16)
            x_half = (jnp.where(lo_half, xp, 0.0), jnp.where(lo_half, 0.0, xp))
            s_half = (jnp.where(lo_half, s_prev_b, 0.0), jnp.where(lo_half, 0.0, s_prev_b))
            y_pair, s_new = None, None
            for hh in range(2):
                h = 2 * p + hh
                lcol = jnp.broadcast_to(lc[:, h:h + 1], (CQ, CQ))
                lmat = jnp.exp2(jnp.where(causal, lcol - lc_t[h:h + 1, :], -jnp.inf))
                m_h = (cb * lmat * dt_t[h:h + 1, :]).astype(BF16)
                c_s = (c_f * jnp.exp2(lcol)).astype(BF16)
                y_h = _dot(m_h, x_half[hh]) + _dot(c_s, s_half[hh])
                w_h = (b_t * w_t[h:h + 1, :]).astype(BF16)
                s_h = _dot(w_h, x_half[hh])
                y_pair = y_h if y_pair is None else y_pair + y_h
                s_new = s_h if s_new is None else s_new + s_h
            y_ref[:, lanes] = y_pair
            st_ref[:, lanes] = s_prev * cdx[0:1, lanes] + s_new

    y = y_ref[...] + xa_ref[...] * dsk_ref[...]
    y = y * _silu(z_ref[rows, :].astype(F32))
    o_ref[rows, :] = _rms(y, nw_ref[...]).astype(BF16)


def _ssd(proj, tail, cwx, cbx, cwb, cbb, dtb, alog, dsk, nw, tri, expand, shift, batch, seq):
    rows = SSD_SUB * CQ
    nc = seq // rows
    m = batch * seq

    def full(shape):
        return pl.BlockSpec(shape, lambda b, c: (0,) * len(shape))

    return pl.pallas_call(
        _ssd_kernel,
        grid=(batch, nc),
        in_specs=[
            pl.BlockSpec((rows, SSD_D_INNER), lambda b, c: (b * nc + c, C_ZSSD // SSD_D_INNER)),
            pl.BlockSpec((rows, SSD_D_INNER), lambda b, c: (b * nc + c, C_XS // SSD_D_INNER)),
            pl.BlockSpec((rows, SSD_BC_DIM), lambda b, c: (b * nc + c, C_BC // SSD_BC_DIM)),
            pl.BlockSpec((rows, LANES), lambda b, c: (b * nc + c, 0)),
            full((SSD_CONV_K, SSD_D_INNER)), full((1, SSD_D_INNER)),
            full((SSD_CONV_K, SSD_BC_DIM)), full((1, SSD_BC_DIM)),
            full((1, LANES)), full((1, LANES)),
            full((1, SSD_D_INNER)), full((1, SSD_D_INNER)),
            full((CQ, CQ)), full((LANES, SSD_D_INNER)), full(((SSD_CONV_K - 1) * CQ, CONV_HIST + CQ)),
        ],
        out_specs=pl.BlockSpec((rows, SSD_D_INNER), lambda b, c: (b * nc + c, 0)),
        out_shape=jax.ShapeDtypeStruct((m, SSD_D_INNER), BF16),
        scratch_shapes=[
            pltpu.VMEM((CONV_HIST + CQ, SSD_D_INNER), BF16),
            pltpu.VMEM((CONV_HIST + CQ, SSD_BC_DIM), BF16),
            pltpu.VMEM((CQ, SSD_D_INNER), F32),
            pltpu.VMEM((CQ, SSD_BC_DIM), F32),
            pltpu.VMEM((SSD_D_STATE, SSD_D_INNER), F32),
            pltpu.VMEM((CQ, SSD_D_INNER), F32),
        ],
        compiler_params=_params(("arbitrary", "arbitrary")),
    )(proj, proj, proj, tail, cwx, cbx, cwb, cbb, dtb, alog, dsk, nw, tri, expand, shift)


N_CMP_PAD = 128
CMP_HALF = CMP_BLOCK // CMP_STRIDE


def _cmp_kernel(kv_ref, pek_ref, w1k_ref, b1k_ref, w2k_ref, pev_ref, w1v_ref, b1v_ref, w2v_ref, knw_ref,
                kc_ref, vc_ref, buf):
    hd = NSA_HEAD_DIM
    for slab in range(2 * NSA_KV_GROUPS):
        buf[slab] = kv_ref[:, slab * hd:(slab + 1) * hd].astype(F32)
    for kind, (pe_ref, w1_ref, b1_ref, w2_ref, out_ref) in enumerate((
            (pek_ref, w1k_ref, b1k_ref, w2k_ref, kc_ref), (pev_ref, w1v_ref, b1v_ref, w2v_ref, vc_ref))):
        for g in range(NSA_KV_GROUPS):
            slab = kind * NSA_KV_GROUPS + g
            u = [jnp.zeros((N_CMP_PAD, hd), F32) for _ in range(CMP_HALF)]
            for r in range(CMP_STRIDE):
                xr = buf[slab, pl.ds(r, N_CMP_PAD, stride=CMP_STRIDE), :]
                for a in range(CMP_HALF):
                    l = a * CMP_STRIDE + r
                    u[a] = u[a] + _dot((xr + pe_ref[l:l + 1, :]).astype(BF16), w1_ref[l * hd:(l + 1) * hd, :])
            acc = u[0] + pltpu.roll(u[1], N_CMP_PAD - 1, 0) + b1_ref[...]
            out = _dot(_silu(acc).astype(BF16), w2_ref[...])
            if kind == 0:
                out_ref[0, :, g * hd:(g + 1) * hd] = _rms(out, knw_ref[...]).astype(BF16)
            else:
                out_ref[0, g * hd:(g + 1) * hd, :] = out.T.astype(BF16)


def _compress(proj, pek, w1k, b1k, w2k, pev, w1v, b1v, w2v, knw, batch, seq):
    hd = NSA_HEAD_DIM

    def full(shape):
        return pl.BlockSpec(shape, lambda b: (0,) * len(shape))

    return pl.pallas_call(
        _cmp_kernel,
        grid=(batch,),
        in_specs=[
            pl.BlockSpec((seq, 2 * NSA_KV_D), lambda b: (b, C_KVCMP // (2 * NSA_KV_D))),
            full((CMP_BLOCK, hd)), full((CMP_BLOCK * hd, hd)), full((1, hd)), full((hd, hd)),
            full((CMP_BLOCK, hd)), full((CMP_BLOCK * hd, hd)), full((1, hd)), full((hd, hd)),
            full((1, hd)),
        ],
        out_specs=[pl.BlockSpec((1, N_CMP_PAD, NSA_KV_D), lambda b: (b, 0, 0)),
                   pl.BlockSpec((1, NSA_KV_D, N_CMP_PAD), lambda b: (b, 0, 0))],
        out_shape=[jax.ShapeDtypeStruct((batch, N_CMP_PAD, NSA_KV_D), BF16),
                   jax.ShapeDtypeStruct((batch, NSA_KV_D, N_CMP_PAD), BF16)],
        scratch_shapes=[pltpu.VMEM((2 * NSA_KV_GROUPS, seq, hd), F32)],
        compiler_params=_params(("arbitrary",)),
    )(proj, pek, w1k, b1k, w2k, pev, w1v, b1v, w2v, knw)


QB = Q_BLOCK
GROWS = NSA_HPG * QB
NSA_SUB = 4
TK = 256
BR_SEL, BR_WIN = 0, 1
N_SLC = 32
N_SLC_PAD = 128
NORM_RC = 256
KT = 128
PW = 2 * QB
N_HPAIRS = NSA_HPG // 2
KX = 2 * NSA_HEAD_DIM
VT_ROWS = NSA_HEAD_DIM + 16
LOG2E = 1.4426950408889634


def _nsa_kernel(*refs):
    def body(sub, carry):
        _nsa_block(sub, *refs)
        return carry

    lax.fori_loop(0, NSA_SUB, body, 0)


def _nsa_block(sub, q_ref, kvs_ref, kvw_ref, kc_ref, vct_ref, z_ref, tail_ref, qnw_ref, ksnw_ref,
               kwnw_ref, selmap_ref, o_ref, kx_s, vt_s, qt_s, oc_s, gt_s, m_s, acc_s, s_s, p_s):
    i = pl.program_id(1) * NSA_SUB + sub
    rows = pl.ds(pl.multiple_of(sub * QB, QB), QB)
    qs = i * QB
    hd = NSA_HEAD_DIM
    seq = kvs_ref.shape[0]

    def prepare(q_r, tail_r, q0):
        qw = qnw_ref[...] * (hd ** -0.5 * LOG2E)
        for h in range(NSA_HEADS):
            qh = q_r[rows, h * hd:(h + 1) * hd].astype(F32)
            qt_s[0:hd, h * QB:(h + 1) * QB] = _rms(qh, qw).T.astype(BF16)
        gt_s[...] = _sigmoid(tail_r[rows, :]).T

        c_idx = lax.broadcasted_iota(jnp.int32, (N_CMP_PAD, GROWS), 0)
        q_in = lax.broadcasted_iota(jnp.int32, (N_CMP_PAD, GROWS), 1) & (QB - 1)
        mask_c = c_idx * CMP_STRIDE + (CMP_BLOCK - 1) <= q0 + q_in
        any_c = CMP_BLOCK - 1 <= q0 + q_in[0:1, :]
        jj = lax.broadcasted_iota(jnp.int32, (N_SLC, QB), 0)
        tq_t = q0 + lax.broadcasted_iota(jnp.int32, (N_SLC, QB), 1)

        for g in range(NSA_KV_GROUPS):
            gl = slice(g * hd, (g + 1) * hd)
            gcols = slice(g * GROWS, (g + 1) * GROWS)
            s = _dot(kc_ref[0, :, gl], qt_s[0:hd, gcols])
            s = jnp.where(mask_c, s, NEG_INF)
            e = jnp.exp2(s - jnp.max(s, axis=0, keepdims=True))
            p_cmp = e * jnp.where(any_c, 1.0 / jnp.sum(e, axis=0, keepdims=True), 0.0)
            oc_s[:, gcols] = _dot(vct_ref[0, gl, :], p_cmp.astype(BF16))

            p_sum = p_cmp[:, 0:QB]
            for k in range(1, NSA_HPG):
                p_sum = p_sum + p_cmp[:, k * QB:(k + 1) * QB]
            imp = sum(_dot(selmap_ref[...], part) for part in _split3(p_sum))[0:N_SLC, :]
            forced = (jj == (tq_t >> 6)) | (jj == 0)
            imp = jnp.where(forced, FORCED_SCORE, jnp.where(jj * SEL_BLOCK <= tq_t, imp, -1.0))
            row_tiles = [slice(r, r + SUBLANES) for r in range(0, N_SLC, SUBLANES)]
            ranks = [jnp.zeros((SUBLANES, QB), jnp.int32) for _ in row_tiles]
            for j2 in range(N_SLC):
                other = imp[j2:j2 + 1, :]
                for r, rs in enumerate(row_tiles):
                    if rs.stop <= j2:
                        beats = other > imp[rs, :]
                    elif rs.start > j2:
                        beats = other >= imp[rs, :]
                    else:
                        beats = (other > imp[rs, :]) | ((other == imp[rs, :]) & (jj[rs, :] > j2))
                    ranks[r] = ranks[r] + jnp.where(beats, 1, 0)
            rank = jnp.concatenate(ranks, axis=0)
            picked = (rank < SEL_TOPK) & (jj * SEL_BLOCK <= tq_t)
            sel_bias = jnp.concatenate([jnp.where(picked, 0.0, NEG_INF).astype(F32),
                                        jnp.zeros((hd - N_SLC, QB), F32)], axis=0).astype(BF16)
            for k in range(NSA_HPG):
                qt_s[hd:KX, (g * NSA_HPG + k) * QB:(g * NSA_HPG + k + 1) * QB] = sel_bias

    @pl.when(i == 0)
    def _():
        for br, src, nw in ((BR_SEL, kvs_ref, ksnw_ref), (BR_WIN, kvw_ref, kwnw_ref)):
            for g in range(NSA_KV_GROUPS):
                def body(r, carry, br=br, src=src, nw=nw, g=g):
                    r0 = pl.multiple_of(r * NORM_RC, NORM_RC)
                    kk = src[pl.ds(r0, NORM_RC), g * hd:(g + 1) * hd].astype(F32)
                    kx_s[br, pl.ds(r0, NORM_RC), g * KX:g * KX + hd] = _rms(kk, nw[...]).astype(BF16)
                    blk = (r0 + lax.broadcasted_iota(jnp.int32, (NORM_RC, hd), 0)) >> 6
                    hit = blk == lax.broadcasted_iota(jnp.int32, (NORM_RC, hd), 1)
                    aux = jnp.where(hit, 1.0 if br == BR_SEL else 0.0, 0.0)
                    kx_s[br, pl.ds(r0, NORM_RC), g * KX + hd:(g + 1) * KX] = aux.astype(BF16)
                    return carry

                lax.fori_loop(0, seq // NORM_RC, body, 0)

                def body_t(r, carry, br=br, src=src, g=g):
                    r0 = pl.multiple_of(r * KT, KT)
                    vv = src[pl.ds(r0, KT), NSA_KV_D + g * hd:NSA_KV_D + (g + 1) * hd].astype(F32)
                    vt_s[br, r, g, 0:hd, :] = vv.T.astype(BF16)
                    vt_s[br, r, g, hd:VT_ROWS, :] = jnp.ones((VT_ROWS - hd, KT), BF16)
                    return carry

                lax.fori_loop(0, seq // KT, body_t, 0)

    prepare(q_ref, tail_ref, qs)

    n_sel = (qs + QB + TK - 1) // TK
    win_first = jnp.maximum(qs - (WINDOW - 1), 0) // TK
    n_win_mid = jnp.maximum(n_sel - win_first - 2, 0)
    n_plain = n_sel - 1 + n_win_mid
    n_tiles = n_plain + 2 + jnp.where(n_sel - win_first >= 2, 1, 0)

    def tile_of(u):
        e = u - n_plain
        br = jnp.where(u < n_plain, jnp.where(u < n_sel - 1, BR_SEL, BR_WIN), jnp.where(e == 0, BR_SEL, BR_WIN))
        t = jnp.where(u < n_plain, jnp.where(u < n_sel - 1, u, win_first + 1 + (u - (n_sel - 1))),
                      jnp.where(e == 1, win_first, n_sel - 1))
        return br, t

    rel0 = (lax.broadcasted_iota(jnp.int32, (TK, QB), 0) - lax.broadcasted_iota(jnp.int32, (TK, QB), 1)) - qs

    m_s[...] = jnp.full_like(m_s, NEG_INF)
    acc_s[...] = jnp.zeros_like(acc_s)
    units = [(g, slice((g * N_HPAIRS + pr) * PW, (g * N_HPAIRS + pr + 1) * PW))
             for g in range(NSA_KV_GROUPS) for pr in range(N_HPAIRS)]

    def keys_of(br, t):
        k0 = pl.multiple_of(t * TK, TK)
        return [kx_s[br, pl.ds(k0, TK), g * KX:(g + 1) * KX] for g in range(NSA_KV_GROUPS)]

    def values_of(br, t):
        return [jnp.concatenate([vt_s[br, t * (TK // KT) + u, g] for u in range(TK // KT)], axis=1)
                for g in range(NSA_KV_GROUPS)]

    def finish_unit(br, v_tt, g, cols, alpha):
        acc_s[br, :, cols] = alpha * acc_s[br, :, cols] + _dot(v_tt[g], p_s[:, cols])

    def finish(br, t, alphas):
        v_tt = values_of(br, t)
        for (g, cols), alpha in zip(units, alphas):
            finish_unit(br, v_tt, g, cols, alpha)

    def tile_step(u, carry, edge, prefetch=True):
        alphas_prev, br_prev, t_prev = carry
        br, t = tile_of(u)
        if prefetch:
            k_next = keys_of(*tile_of(u + 1))
        v_prev = values_of(br_prev, t_prev)
        if edge:
            rel = rel0 + t * TK
            ok = (rel <= 0) & (rel > jnp.where(br == BR_WIN, -WINDOW, -(1 << 30)))
            bias = jnp.where(ok, 0.0, NEG_INF)
            bias = jnp.concatenate([bias, bias], axis=1)
        alphas = []
        for (g, cols), alpha_prev in zip(units, alphas_prev):
            s2 = s_s[:, cols]
            if edge:
                s2 = s2 + bias
            if prefetch:
                s_s[:, cols] = _dot(k_next[g], qt_s[:, cols])
            finish_unit(br_prev, v_prev, g, cols, alpha_prev)
            m_prev = m_s[br, :, cols]
            m_new = jnp.maximum(m_prev, jnp.max(s2, axis=0, keepdims=True))
            m_s[br, :, cols] = m_new
            p_s[:, cols] = jnp.exp2(s2 - m_new).astype(BF16)
            alphas.append(jnp.exp2(m_prev - m_new))
        return tuple(alphas), br, t

    k_first = keys_of(*tile_of(0))
    for g, cols in units:
        s_s[:, cols] = _dot(k_first[g], qt_s[:, cols])
    p_s[...] = jnp.zeros_like(p_s)
    carry = (tuple(jnp.ones((1, PW), F32) for _ in units), jnp.int32(BR_SEL), jnp.int32(0))
    carry = lax.fori_loop(0, n_plain, lambda u, c: tile_step(u, c, edge=False), carry)
    carry = lax.fori_loop(n_plain, n_tiles - 1, lambda u, c: tile_step(u, c, edge=True), carry)
    alphas_last, br_last, t_last = tile_step(n_tiles - 1, carry, edge=True, prefetch=False)
    finish(br_last, t_last, alphas_last)

    for g, cols in units:
        heads = [cols.start // QB + u for u in range(2)]

        def gate_row(branch):
            return jnp.concatenate([gt_s[T_GATE + h * N_NSA_BRANCHES + branch:
                                         T_GATE + h * N_NSA_BRANCHES + branch + 1, :] for h in heads], axis=1)

        o_t = (gate_row(0) * oc_s[:, cols]
               + (gate_row(1) / acc_s[BR_SEL, hd:hd + 1, cols]) * acc_s[BR_SEL, 0:hd, cols]
               + (gate_row(2) / acc_s[BR_WIN, hd:hd + 1, cols]) * acc_s[BR_WIN, 0:hd, cols])
        for u, h in enumerate(heads):
            zz = z_ref[rows, h * hd:(h + 1) * hd].astype(F32)
            o_ref[rows, h * hd:(h + 1) * hd] = (o_t[:, u * QB:(u + 1) * QB].T * _silu(zz)).astype(BF16)


def _nsa(proj, tail, kc, vc, qnw, ksnw, kwnw, selmap_t, batch, seq):
    rows = NSA_SUB * QB
    nq = seq // rows
    hd = NSA_HEAD_DIM
    m = batch * seq

    def full(shape):
        return pl.BlockSpec(shape, lambda b, i: (0,) * len(shape))

    return pl.pallas_call(
        _nsa_kernel,
        grid=(batch, nq),
        in_specs=[
            pl.BlockSpec((rows, NSA_D), lambda b, i: (b * nq + i, C_Q // NSA_D)),
            pl.BlockSpec((seq, 2 * NSA_KV_D), lambda b, i: (b, C_KVSLC // (2 * NSA_KV_D))),
            pl.BlockSpec((seq, 2 * NSA_KV_D), lambda b, i: (b, C_KVWIN // (2 * NSA_KV_D))),
            pl.BlockSpec((1, N_CMP_PAD, NSA_KV_D), lambda b, i: (b, 0, 0)),
            pl.BlockSpec((1, NSA_KV_D, N_CMP_PAD), lambda b, i: (b, 0, 0)),
            pl.BlockSpec((rows, NSA_D), lambda b, i: (b * nq + i, C_ZNSA // NSA_D)),
            pl.BlockSpec((rows, LANES), lambda b, i: (b * nq + i, 0)),
            full((1, hd)), full((1, hd)), full((1, hd)),
            full((N_SLC_PAD, N_CMP_PAD)),
        ],
        out_specs=pl.BlockSpec((rows, NSA_D), lambda b, i: (b * nq + i, 0)),
        out_shape=jax.ShapeDtypeStruct((m, NSA_D), BF16),
        scratch_shapes=[
            pltpu.VMEM((2, seq, NSA_KV_GROUPS * KX), BF16),
            pltpu.VMEM((2, seq // KT, NSA_KV_GROUPS, VT_ROWS, KT), BF16),
            pltpu.VMEM((KX, NSA_HEADS * QB), BF16),
            pltpu.VMEM((hd, NSA_HEADS * QB), F32),
            pltpu.VMEM((LANES, QB), F32),
            pltpu.VMEM((2, 1, NSA_HEADS * QB), F32),
            pltpu.VMEM((2, VT_ROWS, NSA_HEADS * QB), F32),
            pltpu.VMEM((TK, NSA_HEADS * QB), F32),
            pltpu.VMEM((TK, NSA_HEADS * QB), BF16),
        ],
        compiler_params=_params(("arbitrary", "arbitrary")),
    )(proj, proj, proj, kc, vc, proj, tail, qnw, ksnw, kwnw, selmap_t)


OUT_TM = 1024
OUT_TN = 512
MRG_TN = 2 * OUT_TN


def _merge_kernel(ys_ref, yn_ref, gs_ref, gn_ref, ws_ref, wn_ref, o_ref):
    for c in range(MRG_TN // OUT_TN):
        cs = slice(c * OUT_TN, (c + 1) * OUT_TN)
        a = _dot(ys_ref[...], ws_ref[:, cs])
        b = _dot(yn_ref[...], wn_ref[:, cs])
        merged = _sigmoid(gs_ref[:, cs].astype(F32)) * a + _sigmoid(gn_ref[:, cs].astype(F32)) * b
        o_ref[:, cs] = merged.astype(BF16)


def _merge(y_ssd, y_nsa, proj, w_s, w_n):
    m = y_ssd.shape[0]
    nb = D_MODEL // MRG_TN
    return pl.pallas_call(
        _merge_kernel,
        grid=(nb, m // OUT_TM),
        in_specs=[
            pl.BlockSpec((OUT_TM, D_MODEL), lambda j, i: (i, 0)),
            pl.BlockSpec((OUT_TM, D_MODEL), lambda j, i: (i, 0)),
            pl.BlockSpec((OUT_TM, MRG_TN), lambda j, i: (i, C_GLSSD // MRG_TN + j)),
            pl.BlockSpec((OUT_TM, MRG_TN), lambda j, i: (i, C_GLNSA // MRG_TN + j)),
            pl.BlockSpec((D_MODEL, MRG_TN), lambda j, i: (0, j)),
            pl.BlockSpec((D_MODEL, MRG_TN), lambda j, i: (0, j)),
        ],
        out_specs=pl.BlockSpec((OUT_TM, MRG_TN), lambda j, i: (i, j)),
        out_shape=jax.ShapeDtypeStruct((m, D_MODEL), BF16),
        compiler_params=_params(("arbitrary", "arbitrary")),
    )(y_ssd, y_nsa, proj, proj, w_s, w_n)


FIN_TM = 512


def _final_kernel(m_ref, x_ref, w_ref, o_ref):
    for j in range(D_MODEL // OUT_TN):
        cs = slice(j * OUT_TN, (j + 1) * OUT_TN)
        o_ref[:, cs] = x_ref[:, cs] + _dot(m_ref[...], w_ref[:, cs])


def _final(merged, x2, w_o):
    m = merged.shape[0]
    return pl.pallas_call(
        _final_kernel,
        grid=(m // FIN_TM,),
        in_specs=[
            pl.BlockSpec((FIN_TM, D_MODEL), lambda i: (i, 0)),
            pl.BlockSpec((FIN_TM, D_MODEL), lambda i: (i, 0)),
            pl.BlockSpec((D_MODEL, D_MODEL), lambda i: (0, 0)),
        ],
        out_specs=pl.BlockSpec((FIN_TM, D_MODEL), lambda i: (i, 0)),
        out_shape=jax.ShapeDtypeStruct((m, D_MODEL), F32),
        compiler_params=_params(("arbitrary",)),
    )(merged, x2, w_o)


def _w_in_sections(w_in):
    sec = {name: w_in[:, _OFFS[i]:_OFFS[i + 1]] for i, name in enumerate(
        ("z_ssd", "xbc", "dt", "q", "k_cmp", "v_cmp", "k_slc", "v_slc", "k_win", "v_win",
         "z_nsa", "gates", "gl_ssd", "gl_nsa"))}
    main = jnp.concatenate([
        sec["z_ssd"], sec["xbc"][:, :SSD_D_INNER], sec["q"], sec["z_nsa"], sec["gl_ssd"], sec["gl_nsa"],
        sec["xbc"][:, SSD_D_INNER:], sec["k_cmp"], sec["v_cmp"], sec["k_slc"], sec["v_slc"],
        sec["k_win"], sec["v_win"], jnp.zeros((D_MODEL, N_PAD - N_MAIN), w_in.dtype)], axis=1).astype(BF16)
    pad = LANES - SSD_HEADS - NSA_HEADS * N_NSA_BRANCHES
    tail = jnp.concatenate([sec["dt"], sec["gates"], jnp.zeros((D_MODEL, pad), w_in.dtype)], axis=1).astype(BF16)
    return main, tail


def _constants():
    tri = np.tril(np.ones((CQ, CQ), np.float32))
    expand = np.zeros((LANES, SSD_D_INNER), np.float32)
    for h in range(SSD_HEADS):
        expand[h, h * SSD_HEAD_DIM:(h + 1) * SSD_HEAD_DIM] = 1.0
    n_cmp = N_CMP_PAD - 1
    ci = np.arange(n_cmp)[:, None] * CMP_STRIDE
    sj = np.arange(32)[None, :] * SEL_BLOCK
    sel_map = ((ci < sj + SEL_BLOCK) & (ci + CMP_BLOCK > sj)).astype(np.float32)
    selmap_t = np.zeros((N_SLC_PAD, N_CMP_PAD), np.float32)
    selmap_t[:32, :n_cmp] = sel_map.T
    shift = np.zeros(((SSD_CONV_K - 1) * CQ, CONV_HIST + CQ), np.float32)
    for k in range(SSD_CONV_K - 1):
        for t in range(CQ):
            shift[k * CQ + t, CONV_HIST + t - (SSD_CONV_K - 1) + k] = 1.0
    return jnp.asarray(tri, BF16), jnp.asarray(expand, BF16), jnp.asarray(selmap_t, BF16), jnp.asarray(shift, BF16)


def _pad_lanes(v):
    return jnp.concatenate([v.astype(F32), jnp.zeros((LANES - v.shape[0],), F32)])[None, :]


def kernel(x, norm_w, w_in, conv_w, conv_b, dt_bias, a_log, d_skip, ssd_norm_w, q_norm_w, k_cmp_norm_w,
           k_slc_norm_w, k_win_norm_w, cmp_pe_k, cmp_w1_k, cmp_b1_k, cmp_w2_k, cmp_pe_v, cmp_w1_v, cmp_b1_v,
           cmp_w2_v, w_out_ssd, w_out_nsa, w_o):
    batch, seq, _ = x.shape
    assert seq == 2048 and (batch * seq) % IN_TM == 0
    x2 = x.reshape(batch * seq, D_MODEL)
    w_main, w_tail = _w_in_sections(w_in)
    tri, expand, selmap_t, shift = _constants()

    proj, tail = _inproj(x2, norm_w[None, :], w_main, w_tail)

    y_ssd = _ssd(proj, tail,
                 conv_w[:, :SSD_D_INNER], conv_b[None, :SSD_D_INNER],
                 conv_w[:, SSD_D_INNER:], conv_b[None, SSD_D_INNER:],
                 _pad_lanes(dt_bias), _pad_lanes(a_log),
                 jnp.repeat(d_skip, SSD_HEAD_DIM)[None, :], ssd_norm_w[None, :],
                 tri, expand, shift, batch, seq)

    kc, vc = _compress(proj, cmp_pe_k, cmp_w1_k.astype(BF16), cmp_b1_k[None, :], cmp_w2_k.astype(BF16),
                       cmp_pe_v, cmp_w1_v.astype(BF16), cmp_b1_v[None, :], cmp_w2_v.astype(BF16),
                       k_cmp_norm_w[None, :], batch, seq)

    y_nsa = _nsa(proj, tail, kc, vc, q_norm_w[None, :], k_slc_norm_w[None, :], k_win_norm_w[None, :],
                 selmap_t, batch, seq)

    merged = _merge(y_ssd, y_nsa, proj, w_out_ssd.astype(BF16), w_out_nsa.astype(BF16))
    out = _final(merged, x2, w_o.astype(BF16))
    return out.reshape(batch, seq, D_MODEL)
```

```python
import functools

import numpy as np
import jax
import jax.numpy as jnp
from jax import lax
from jax.experimental import pallas as pl
from jax.experimental.pallas import tpu as pltpu

D_MODEL = 2048
SSD_D_INNER = D_MODEL
SSD_HEAD_DIM = 64
SSD_HEADS = SSD_D_INNER // SSD_HEAD_DIM
SSD_GROUPS = 4
SSD_D_STATE = 128
SSD_CONV_K = 4
SSD_CHUNK = 128
SSD_BC_DIM = 2 * SSD_GROUPS * SSD_D_STATE
SSD_CONV_DIM = SSD_D_INNER + SSD_BC_DIM

NSA_HEADS = 16
NSA_HEAD_DIM = 128
NSA_KV_GROUPS = 2
NSA_HPG = NSA_HEADS // NSA_KV_GROUPS
NSA_D = NSA_HEADS * NSA_HEAD_DIM
NSA_KV_D = NSA_KV_GROUPS * NSA_HEAD_DIM
N_NSA_BRANCHES = 3
CMP_BLOCK = 32
CMP_STRIDE = 16
SEL_BLOCK = 64
SEL_TOPK = 8
WINDOW = 512
Q_BLOCK = 128

EPS = 1e-6
NEG_INF = -1e30
FORCED_SCORE = 1e9

LANES = 128
SUBLANES = 8
VMEM_LIMIT = 56 * 1024 * 1024

F32 = jnp.float32
BF16 = jnp.bfloat16
HIGHEST = lax.Precision.HIGHEST

C_ZSSD = 0
C_XS = 2048
C_Q = 4096
C_ZNSA = 6144
C_GLSSD = 8192
C_GLNSA = 10240
C_BC = 12288
C_KVCMP = 13312
C_KVSLC = 13824
C_KVWIN = 14336
N_MAIN = 14848
T_GATE = SSD_HEADS

_SIZES = [SSD_D_INNER, SSD_CONV_DIM, SSD_HEADS, NSA_D] + [NSA_KV_D] * 6 + [NSA_D, NSA_HEADS * N_NSA_BRANCHES, D_MODEL, D_MODEL]
_OFFS = [0] + [int(o) for o in np.cumsum(_SIZES)]


def _sigmoid(v):
    return 0.5 + 0.5 * jnp.tanh(0.5 * v)


def _silu(v):
    h = 0.5 * v
    return h + h * jnp.tanh(h)


def _softplus(v):
    return jnp.maximum(v, 0.0) + jnp.log(1.0 + jnp.exp(-jnp.abs(v)))


def _rms(v, w):
    ms = jnp.mean(v * v, axis=-1, keepdims=True)
    return v * lax.rsqrt(ms + EPS) * w


def _dot(a, b, precision=None):
    return jnp.dot(a, b, preferred_element_type=F32, precision=precision)


def _split3(v):
    hi = v.astype(BF16)
    r1 = v - hi.astype(F32)
    mid = r1.astype(BF16)
    lo = (r1 - mid.astype(F32)).astype(BF16)
    return hi, mid, lo


def _dot_nt(a, b, precision=None):
    return lax.dot_general(a, b, (((1,), (1,)), ((), ())), preferred_element_type=F32, precision=precision)


def _params(sem):
    return pltpu.CompilerParams(dimension_semantics=sem, vmem_limit_bytes=VMEM_LIMIT)


IN_TM = 1024
IN_CHUNK = 512
IN_TN = 4 * IN_CHUNK
N_PAD = -(-N_MAIN // IN_TN) * IN_TN
IN_RC = 256


def _inproj_kernel(x_ref, nw_ref, w_ref, wt_ref, o_ref, t_ref, h_ref):
    j = pl.program_id(1)

    @pl.when(j == 0)
    def _():
        def body(r, carry):
            r0 = pl.multiple_of(r * IN_RC, IN_RC)
            h_ref[pl.ds(r0, IN_RC), :] = _rms(x_ref[pl.ds(r0, IN_RC), :], nw_ref[...]).astype(BF16)
            return carry

        lax.fori_loop(0, IN_TM // IN_RC, body, 0)
        t_ref[...] = _dot(h_ref[...], wt_ref[...])

    def chunks(n_valid):
        for c in range(IN_TN // IN_CHUNK):
            cs = slice(c * IN_CHUNK, (c + 1) * IN_CHUNK)
            if c < n_valid:
                o_ref[:, cs] = _dot(h_ref[...], w_ref[:, cs]).astype(BF16)
            else:
                o_ref[:, cs] = jnp.zeros((IN_TM, IN_CHUNK), BF16)

    last = N_PAD // IN_TN - 1
    pl.when(j < last)(lambda: chunks(IN_TN // IN_CHUNK))
    pl.when(j == last)(lambda: chunks((N_MAIN - last * IN_TN) // IN_CHUNK))


def _inproj(x2, norm_w, w_main, w_tail):
    m = x2.shape[0]
    return pl.pallas_call(
        _inproj_kernel,
        grid=(m // IN_TM, N_PAD // IN_TN),
        in_specs=[
            pl.BlockSpec((IN_TM, D_MODEL), lambda i, j: (i, 0)),
            pl.BlockSpec((1, D_MODEL), lambda i, j: (0, 0)),
            pl.BlockSpec((D_MODEL, IN_TN), lambda i, j: (0, j)),
            pl.BlockSpec((D_MODEL, LANES), lambda i, j: (0, 0)),
        ],
        out_specs=[
            pl.BlockSpec((IN_TM, IN_TN), lambda i, j: (i, j)),
            pl.BlockSpec((IN_TM, LANES), lambda i, j: (i, 0)),
        ],
        out_shape=[
            jax.ShapeDtypeStruct((m, N_PAD), BF16),
            jax.ShapeDtypeStruct((m, LANES), F32),
        ],
        scratch_shapes=[pltpu.VMEM((IN_TM, D_MODEL), BF16)],
        compiler_params=_params(("arbitrary", "arbitrary")),
    )(x2, norm_w, w_main, w_tail)


CQ = SSD_CHUNK
PAIR = 2 * SSD_HEAD_DIM
N_PAIRS = SSD_HEADS // 2
PAIRS_PER_GROUP = N_PAIRS // SSD_GROUPS
SSD_SUB = 4
CONV_HIST = 16
CONV_COLS = 256


def _ssd_kernel(*refs):
    xbuf, bbuf, st_ref = refs[16], refs[17], refs[20]

    @pl.when(pl.program_id(1) == 0)
    def _():
        xbuf[0:CONV_HIST, :] = jnp.zeros((CONV_HIST, SSD_D_INNER), BF16)
        bbuf[0:CONV_HIST, :] = jnp.zeros((CONV_HIST, SSD_BC_DIM), BF16)
        st_ref[...] = jnp.zeros_like(st_ref)

    def body(sub, carry):
        _ssd_chunk(pl.ds(pl.multiple_of(sub * CQ, CQ), CQ), *refs)
        return carry

    lax.fori_loop(0, SSD_SUB, body, 0)


def _ssd_chunk(rows, z_ref, xs_ref, bc_ref, tail_ref, cwx_ref, cbx_ref, cwb_ref, cbb_ref, dtb_ref, alog_ref,
               dsk_ref, nw_ref, tri_ref, exp_ref, shift_ref, o_ref, xbuf, bbuf, xa_ref, ba_ref, st_ref, y_ref):
    xbuf[CONV_HIST:CONV_HIST + CQ, :] = xs_ref[rows, :]
    bbuf[CONV_HIST:CONV_HIST + CQ, :] = bc_ref[rows, :]

    def conv(buf, cur_ref, w_ref, b_ref, out_ref):
        for j in range(buf.shape[1] // CONV_COLS):
            cs = slice(j * CONV_COLS, (j + 1) * CONV_COLS)
            shifted = _dot(shift_ref[...], buf[:, cs])
            acc = b_ref[:, cs] + cur_ref[rows, cs].astype(F32) * w_ref[SSD_CONV_K - 1:SSD_CONV_K, cs]
            for k in range(SSD_CONV_K - 1):
                acc = acc + shifted[k * CQ:(k + 1) * CQ, :] * w_ref[k:k + 1, cs]
            out_ref[:, cs] = _silu(acc)

    conv(xbuf, xs_ref, cwx_ref, cbx_ref, xa_ref)
    conv(bbuf, bc_ref, cwb_ref, cbb_ref, ba_ref)
    xbuf[0:CONV_HIST, :] = xbuf[CQ:CQ + CONV_HIST, :]
    bbuf[0:CONV_HIST, :] = bbuf[CQ:CQ + CONV_HIST, :]

    dt = _softplus(tail_ref[rows, :] + dtb_ref[...])
    la = dt * (-jnp.exp(alog_ref[...]))
    lc = sum(_dot(tri_ref[...], part) for part in _split3(la)) * LOG2E
    lc_t = lc.T
    dt_t = dt.T
    w_t = jnp.exp2(lc_t[:, CQ - 1:CQ] - lc_t) * dt_t
    cd = jnp.exp2(lc[CQ - 1:CQ, :])
    cdx = sum(_dot(part, exp_ref[...]) for part in _split3(jnp.broadcast_to(cd, (SUBLANES, LANES))))

    row = lax.broadcasted_iota(jnp.int32, (CQ, CQ), 0)
    col = lax.broadcasted_iota(jnp.int32, (CQ, CQ), 1)
    causal = row >= col
    lo_half = col < SSD_HEAD_DIM

    for g in range(SSD_GROUPS):
        b_f = ba_ref[:, g * SSD_D_STATE:(g + 1) * SSD_D_STATE]
        c_f = ba_ref[:, (SSD_GROUPS + g) * SSD_D_STATE:(SSD_GROUPS + g + 1) * SSD_D_STATE]
        cb = _dot_nt(c_f.astype(BF16), b_f.astype(BF16))
        b_t = b_f.T
        for pp in range(PAIRS_PER_GROUP):
            p = g * PAIRS_PER_GROUP + pp
            lanes = slice(p * PAIR, (p + 1) * PAIR)
            xp = xa_ref[:, lanes].astype(BF16)
            s_prev = st_ref[:, lanes]
            s_prev_b = s_prev.astype(BF16)
            x_half = (jnp.where(lo_half, xp, 0.0), jnp.where(lo_half, 0.0, xp))
            s_half = (jnp.where(lo_half, s_prev_b, 0.0), jnp.where(lo_half, 0.0, s_prev_b))
            y_pair, s_new = None, None
            for hh in range(2):
                h = 2 * p + hh
                lcol = jnp.broadcast_to(lc[:, h:h + 1], (CQ, CQ))
                lmat = jnp.exp2(jnp.where(causal, lcol - lc_t[h:h + 1, :], -jnp.inf))
                m_h = (cb * lmat * dt_t[h:h + 1, :]).astype(BF16)
                c_s = (c_f * jnp.exp2(lcol)).astype(BF16)
                y_h = _dot(m_h, x_half[hh]) + _dot(c_s, s_half[hh])
                w_h = (b_t * w_t[h:h + 1, :]).astype(BF16)
                s_h = _dot(w_h, x_half[hh])
                y_pair = y_h if y_pair is None else y_pair + y_h
                s_new = s_h if s_new is None else s_new + s_h
            y_ref[:, lanes] = y_pair
            st_ref[:, lanes] = s_prev * cdx[0:1, lanes] + s_new

    y = y_ref[...] + xa_ref[...] * dsk_ref[...]
    y = y * _silu(z_ref[rows, :].astype(F32))
    o_ref[rows, :] = _rms(y, nw_ref[...]).astype(BF16)


def _ssd(proj, tail, cwx, cbx, cwb, cbb, dtb, alog, dsk, nw, tri, expand, shift, batch, seq):
    rows = SSD_SUB * CQ
    nc = seq // rows
    m = batch * seq

    def full(shape):
        return pl.BlockSpec(shape, lambda b, c: (0,) * len(shape))

    return pl.pallas_call(
        _ssd_kernel,
        grid=(batch, nc),
        in_specs=[
            pl.BlockSpec((rows, SSD_D_INNER), lambda b, c: (b * nc + c, C_ZSSD // SSD_D_INNER)),
            pl.BlockSpec((rows, SSD_D_INNER), lambda b, c: (b * nc + c, C_XS // SSD_D_INNER)),
            pl.BlockSpec((rows, SSD_BC_DIM), lambda b, c: (b * nc + c, C_BC // SSD_BC_DIM)),
            pl.BlockSpec((rows, LANES), lambda b, c: (b * nc + c, 0)),
            full((SSD_CONV_K, SSD_D_INNER)), full((1, SSD_D_INNER)),
            full((SSD_CONV_K, SSD_BC_DIM)), full((1, SSD_BC_DIM)),
            full((1, LANES)), full((1, LANES)),
            full((1, SSD_D_INNER)), full((1, SSD_D_INNER)),
            full((CQ, CQ)), full((LANES, SSD_D_INNER)), full(((SSD_CONV_K - 1) * CQ, CONV_HIST + CQ)),
        ],
        out_specs=pl.BlockSpec((rows, SSD_D_INNER), lambda b, c: (b * nc + c, 0)),
        out_shape=jax.ShapeDtypeStruct((m, SSD_D_INNER), BF16),
        scratch_shapes=[
            pltpu.VMEM((CONV_HIST + CQ, SSD_D_INNER), BF16),
            pltpu.VMEM((CONV_HIST + CQ, SSD_BC_DIM), BF16),
            pltpu.VMEM((CQ, SSD_D_INNER), F32),
            pltpu.VMEM((CQ, SSD_BC_DIM), F32),
            pltpu.VMEM((SSD_D_STATE, SSD_D_INNER), F32),
            pltpu.VMEM((CQ, SSD_D_INNER), F32),
        ],
        compiler_params=_params(("arbitrary", "arbitrary")),
    )(proj, proj, proj, tail, cwx, cbx, cwb, cbb, dtb, alog, dsk, nw, tri, expand, shift)


N_CMP_PAD = 128
CMP_HALF = CMP_BLOCK // CMP_STRIDE


def _cmp_kernel(kv_ref, pek_ref, w1k_ref, b1k_ref, w2k_ref, pev_ref, w1v_ref, b1v_ref, w2v_ref, knw_ref,
                kc_ref, vc_ref, buf):
    hd = NSA_HEAD_DIM
    for slab in range(2 * NSA_KV_GROUPS):
        buf[slab] = kv_ref[:, slab * hd:(slab + 1) * hd].astype(F32)
    for kind, (pe_ref, w1_ref, b1_ref, w2_ref, out_ref) in enumerate((
            (pek_ref, w1k_ref, b1k_ref, w2k_ref, kc_ref), (pev_ref, w1v_ref, b1v_ref, w2v_ref, vc_ref))):
        for g in range(NSA_KV_GROUPS):
            slab = kind * NSA_KV_GROUPS + g
            u = [jnp.zeros((N_CMP_PAD, hd), F32) for _ in range(CMP_HALF)]
            for r in range(CMP_STRIDE):
                xr = buf[slab, pl.ds(r, N_CMP_PAD, stride=CMP_STRIDE), :]
                for a in range(CMP_HALF):
                    l = a * CMP_STRIDE + r
                    u[a] = u[a] + _dot((xr + pe_ref[l:l + 1, :]).astype(BF16), w1_ref[l * hd:(l + 1) * hd, :])
            acc = u[0] + pltpu.roll(u[1], N_CMP_PAD - 1, 0) + b1_ref[...]
            out = _dot(_silu(acc).astype(BF16), w2_ref[...])
            if kind == 0:
                out_ref[0, :, g * hd:(g + 1) * hd] = _rms(out, knw_ref[...]).astype(BF16)
            else:
                out_ref[0, g * hd:(g + 1) * hd, :] = out.T.astype(BF16)


def _compress(proj, pek, w1k, b1k, w2k, pev, w1v, b1v, w2v, knw, batch, seq):
    hd = NSA_HEAD_DIM

    def full(shape):
        return pl.BlockSpec(shape, lambda b: (0,) * len(shape))

    return pl.pallas_call(
        _cmp_kernel,
        grid=(batch,),
        in_specs=[
            pl.BlockSpec((seq, 2 * NSA_KV_D), lambda b: (b, C_KVCMP // (2 * NSA_KV_D))),
            full((CMP_BLOCK, hd)), full((CMP_BLOCK * hd, hd)), full((1, hd)), full((hd, hd)),
            full((CMP_BLOCK, hd)), full((CMP_BLOCK * hd, hd)), full((1, hd)), full((hd, hd)),
            full((1, hd)),
        ],
        out_specs=[pl.BlockSpec((1, N_CMP_PAD, NSA_KV_D), lambda b: (b, 0, 0)),
                   pl.BlockSpec((1, NSA_KV_D, N_CMP_PAD), lambda b: (b, 0, 0))],
        out_shape=[jax.ShapeDtypeStruct((batch, N_CMP_PAD, NSA_KV_D), BF16),
                   jax.ShapeDtypeStruct((batch, NSA_KV_D, N_CMP_PAD), BF16)],
        scratch_shapes=[pltpu.VMEM((2 * NSA_KV_GROUPS, seq, hd), F32)],
        compiler_params=_params(("arbitrary",)),
    )(proj, pek, w1k, b1k, w2k, pev, w1v, b1v, w2v, knw)


QB = Q_BLOCK
GROWS = NSA_HPG * QB
NSA_SUB = 4
TK = 256
BR_SEL, BR_WIN = 0, 1
N_SLC = 32
N_SLC_PAD = 128
NORM_RC = 1024
KT = 128
VT_PER_TRIP = 4
PW = 2 * QB
N_HPAIRS = NSA_HPG // 2
KX = 2 * NSA_HEAD_DIM
VT_ROWS = NSA_HEAD_DIM + 16
LOG2E = 1.4426950408889634


def _nsa_kernel(*refs):
    def body(sub, carry):
        _nsa_block(sub, *refs)
        return carry

    lax.fori_loop(0, NSA_SUB, body, 0)


def _nsa_block(sub, q_ref, kvs_ref, kvw_ref, kc_ref, vct_ref, z_ref, tail_ref, qnw_ref, ksnw_ref,
               kwnw_ref, selmap_ref, o_ref, kx_s, vt_s, qt_s, oc_s, gt_s, m_s, acc_s, s_s, p_s):
    i = pl.program_id(1) * NSA_SUB + sub
    rows = pl.ds(pl.multiple_of(sub * QB, QB), QB)
    qs = i * QB
    hd = NSA_HEAD_DIM
    seq = kvs_ref.shape[0]

    def prepare(q_r, tail_r, q0):
        qw = qnw_ref[...] * (hd ** -0.5 * LOG2E)
        for h in range(NSA_HEADS):
            qh = q_r[rows, h * hd:(h + 1) * hd].astype(F32)
            qt_s[0:hd, h * QB:(h + 1) * QB] = _rms(qh, qw).T.astype(BF16)
        gt_s[...] = _sigmoid(tail_r[rows, :]).T

        c_idx = lax.broadcasted_iota(jnp.int32, (N_CMP_PAD, GROWS), 0)
        q_in = lax.broadcasted_iota(jnp.int32, (N_CMP_PAD, GROWS), 1) & (QB - 1)
        mask_c = c_idx * CMP_STRIDE + (CMP_BLOCK - 1) <= q0 + q_in
        any_c = CMP_BLOCK - 1 <= q0 + q_in[0:1, :]
        jj = lax.broadcasted_iota(jnp.int32, (N_SLC, QB), 0)
        tq_t = q0 + lax.broadcasted_iota(jnp.int32, (N_SLC, QB), 1)

        for g in range(NSA_KV_GROUPS):
            gl = slice(g * hd, (g + 1) * hd)
            gcols = slice(g * GROWS, (g + 1) * GROWS)
            s = _dot(kc_ref[0, :, gl], qt_s[0:hd, gcols])
            s = jnp.where(mask_c, s, NEG_INF)
            e = jnp.exp2(s - jnp.max(s, axis=0, keepdims=True))
            p_cmp = e * jnp.where(any_c, 1.0 / jnp.sum(e, axis=0, keepdims=True), 0.0)
            oc_s[:, gcols] = _dot(vct_ref[0, gl, :], p_cmp.astype(BF16))

            p_sum = p_cmp[:, 0:QB]
            for k in range(1, NSA_HPG):
                p_sum = p_sum + p_cmp[:, k * QB:(k + 1) * QB]
            imp = sum(_dot(selmap_ref[...], part) for part in _split3(p_sum))[0:N_SLC, :]
            forced = (jj == (tq_t >> 6)) | (jj == 0)
            imp = jnp.where(forced, FORCED_SCORE, jnp.where(jj * SEL_BLOCK <= tq_t, imp, -1.0))
            row_tiles = [slice(r, r + SUBLANES) for r in range(0, N_SLC, SUBLANES)]
            ranks = [jnp.zeros((SUBLANES, QB), jnp.int32) for _ in row_tiles]
            for j2 in range(N_SLC):
                other = imp[j2:j2 + 1, :]
                for r, rs in enumerate(row_tiles):
                    if rs.stop <= j2:
                        beats = other > imp[rs, :]
                    elif rs.start > j2:
                        beats = other >= imp[rs, :]
                    else:
                        beats = (other > imp[rs, :]) | ((other == imp[rs, :]) & (jj[rs, :] > j2))
                    ranks[r] = ranks[r] + jnp.where(beats, 1, 0)
            rank = jnp.concatenate(ranks, axis=0)
            picked = (rank < SEL_TOPK) & (jj * SEL_BLOCK <= tq_t)
            sel_bias = jnp.concatenate([jnp.where(picked, 0.0, NEG_INF).astype(F32),
                                        jnp.zeros((hd - N_SLC, QB), F32)], axis=0).astype(BF16)
            for k in range(NSA_HPG):
                qt_s[hd:KX, (g * NSA_HPG + k) * QB:(g * NSA_HPG + k + 1) * QB] = sel_bias

    @pl.when(i == 0)
    def _():
        for br, src, nw in ((BR_SEL, kvs_ref, ksnw_ref), (BR_WIN, kvw_ref, kwnw_ref)):
            for g in range(NSA_KV_GROUPS):
                def body(r, carry, br=br, src=src, nw=nw, g=g):
                    r0 = pl.multiple_of(r * NORM_RC, NORM_RC)
                    kk = src[pl.ds(r0, NORM_RC), g * hd:(g + 1) * hd].astype(F32)
                    kx_s[br, pl.ds(r0, NORM_RC), g * KX:g * KX + hd] = _rms(kk, nw[...]).astype(BF16)
                    blk = (r0 + lax.broadcasted_iota(jnp.int32, (NORM_RC, hd), 0)) >> 6
                    hit = blk == lax.broadcasted_iota(jnp.int32, (NORM_RC, hd), 1)
                    aux = jnp.where(hit, 1.0 if br == BR_SEL else 0.0, 0.0)
                    kx_s[br, pl.ds(r0, NORM_RC), g * KX + hd:(g + 1) * KX] = aux.astype(BF16)
                    return carry

                lax.fori_loop(0, seq // NORM_RC, body, 0)

                def body_t(r, carry, br=br, src=src, g=g):
                    for u in range(VT_PER_TRIP):
                        tile = r * VT_PER_TRIP + u
                        r0 = pl.multiple_of(tile * KT, KT)
                        vv = src[pl.ds(r0, KT), NSA_KV_D + g * hd:NSA_KV_D + (g + 1) * hd].astype(F32)
                        vt_s[br, tile, g, 0:hd, :] = vv.T.astype(BF16)
                        vt_s[br, tile, g, hd:VT_ROWS, :] = jnp.ones((VT_ROWS - hd, KT), BF16)
                    return carry

                lax.fori_loop(0, seq // (KT * VT_PER_TRIP), body_t, 0)

    prepare(q_ref, tail_ref, qs)

    n_sel = (qs + QB + TK - 1) // TK
    win_first = jnp.maximum(qs - (WINDOW - 1), 0) // TK
    n_win_mid = jnp.maximum(n_sel - win_first - 2, 0)
    n_plain = n_sel - 1 + n_win_mid
    n_tiles = n_plain + 2 + jnp.where(n_sel - win_first >= 2, 1, 0)

    def tile_of(u):
        e = u - n_plain
        br = jnp.where(u < n_plain, jnp.where(u < n_sel - 1, BR_SEL, BR_WIN), jnp.where(e == 0, BR_SEL, BR_WIN))
        t = jnp.where(u < n_plain, jnp.where(u < n_sel - 1, u, win_first + 1 + (u - (n_sel - 1))),
                      jnp.where(e == 1, win_first, n_sel - 1))
        return br, t

    rel0 = (lax.broadcasted_iota(jnp.int32, (TK, QB), 0) - lax.broadcasted_iota(jnp.int32, (TK, QB), 1)) - qs

    m_s[...] = jnp.full_like(m_s, NEG_INF)
    acc_s[...] = jnp.zeros_like(acc_s)
    units = [(g, slice((g * N_HPAIRS + pr) * PW, (g * N_HPAIRS + pr + 1) * PW))
             for g in range(NSA_KV_GROUPS) for pr in range(N_HPAIRS)]

    def keys_of(br, t):
        k0 = pl.multiple_of(t * TK, TK)
        return [kx_s[br, pl.ds(k0, TK), g * KX:(g + 1) * KX] for g in range(NSA_KV_GROUPS)]

    def values_of(br, t):
        return [jnp.concatenate([vt_s[br, t * (TK // KT) + u, g] for u in range(TK // KT)], axis=1)
                for g in range(NSA_KV_GROUPS)]

    def finish_unit(br, v_tt, g, cols, alpha):
        acc_s[br, :, cols] = alpha * acc_s[br, :, cols] + _dot(v_tt[g], p_s[:, cols])

    def finish(br, t, alphas):
        v_tt = values_of(br, t)
        for (g, cols), alpha in zip(units, alphas):
            finish_unit(br, v_tt, g, cols, alpha)

    def tile_step(u, carry, edge, prefetch=True):
        alphas_prev, br_prev, t_prev = carry
        br, t = tile_of(u)
        if prefetch:
            k_next = keys_of(*tile_of(u + 1))
        v_prev = values_of(br_prev, t_prev)
        if edge:
            rel = rel0 + t * TK
            ok = (rel <= 0) & (rel > jnp.where(br == BR_WIN, -WINDOW, -(1 << 30)))
            bias = jnp.where(ok, 0.0, NEG_INF)
            bias = jnp.concatenate([bias, bias], axis=1)
        alphas = []
        for (g, cols), alpha_prev in zip(units, alphas_prev):
            s2 = s_s[:, cols]
            if edge:
                s2 = s2 + bias
            if prefetch:
                s_s[:, cols] = _dot(k_next[g], qt_s[:, cols])
            finish_unit(br_prev, v_prev, g, cols, alpha_prev)
            m_prev = m_s[br, :, cols]
            m_new = jnp.maximum(m_prev, jnp.max(s2, axis=0, keepdims=True))
            m_s[br, :, cols] = m_new
            p_s[:, cols] = jnp.exp2(s2 - m_new).astype(BF16)
            alphas.append(jnp.exp2(m_prev - m_new))
        return tuple(alphas), br, t

    k_first = keys_of(*tile_of(0))
    for g, cols in units:
        s_s[:, cols] = _dot(k_first[g], qt_s[:, cols])
    p_s[...] = jnp.zeros_like(p_s)
    carry = (tuple(jnp.ones((1, PW), F32) for _ in units), jnp.int32(BR_SEL), jnp.int32(0))
    carry = lax.fori_loop(0, n_plain, lambda u, c: tile_step(u, c, edge=False), carry)
    carry = lax.fori_loop(n_plain, n_tiles - 1, lambda u, c: tile_step(u, c, edge=True), carry)
    alphas_last, br_last, t_last = tile_step(n_tiles - 1, carry, edge=True, prefetch=False)
    finish(br_last, t_last, alphas_last)

    for g, cols in units:
        heads = [cols.start // QB + u for u in range(2)]

        def gate_row(branch):
            return jnp.concatenate([gt_s[T_GATE + h * N_NSA_BRANCHES + branch:
                                         T_GATE + h * N_NSA_BRANCHES + branch + 1, :] for h in heads], axis=1)

        o_t = (gate_row(0) * oc_s[:, cols]
               + (gate_row(1) / acc_s[BR_SEL, hd:hd + 1, cols]) * acc_s[BR_SEL, 0:hd, cols]
               + (gate_row(2) / acc_s[BR_WIN, hd:hd + 1, cols]) * acc_s[BR_WIN, 0:hd, cols])
        for u, h in enumerate(heads):
            zz = z_ref[rows, h * hd:(h + 1) * hd].astype(F32)
            o_ref[rows, h * hd:(h + 1) * hd] = (o_t[:, u * QB:(u + 1) * QB].T * _silu(zz)).astype(BF16)


def _nsa(proj, tail, kc, vc, qnw, ksnw, kwnw, selmap_t, batch, seq):
    rows = NSA_SUB * QB
    nq = seq // rows
    hd = NSA_HEAD_DIM
    m = batch * seq

    def full(shape):
        return pl.BlockSpec(shape, lambda b, i: (0,) * len(shape))

    return pl.pallas_call(
        _nsa_kernel,
        grid=(batch, nq),
        in_specs=[
            pl.BlockSpec((rows, NSA_D), lambda b, i: (b * nq + i, C_Q // NSA_D)),
            pl.BlockSpec((seq, 2 * NSA_KV_D), lambda b, i: (b, C_KVSLC // (2 * NSA_KV_D))),
            pl.BlockSpec((seq, 2 * NSA_KV_D), lambda b, i: (b, C_KVWIN // (2 * NSA_KV_D))),
            pl.BlockSpec((1, N_CMP_PAD, NSA_KV_D), lambda b, i: (b, 0, 0)),
            pl.BlockSpec((1, NSA_KV_D, N_CMP_PAD), lambda b, i: (b, 0, 0)),
            pl.BlockSpec((rows, NSA_D), lambda b, i: (b * nq + i, C_ZNSA // NSA_D)),
            pl.BlockSpec((rows, LANES), lambda b, i: (b * nq + i, 0)),
            full((1, hd)), full((1, hd)), full((1, hd)),
            full((N_SLC_PAD, N_CMP_PAD)),
        ],
        out_specs=pl.BlockSpec((rows, NSA_D), lambda b, i: (b * nq + i, 0)),
        out_shape=jax.ShapeDtypeStruct((m, NSA_D), BF16),
        scratch_shapes=[
            pltpu.VMEM((2, seq, NSA_KV_GROUPS * KX), BF16),
            pltpu.VMEM((2, seq // KT, NSA_KV_GROUPS, VT_ROWS, KT), BF16),
            pltpu.VMEM((KX, NSA_HEADS * QB), BF16),
            pltpu.VMEM((hd, NSA_HEADS * QB), F32),
            pltpu.VMEM((LANES, QB), F32),
            pltpu.VMEM((2, 1, NSA_HEADS * QB), F32),
            pltpu.VMEM((2, VT_ROWS, NSA_HEADS * QB), F32),
            pltpu.VMEM((TK, NSA_HEADS * QB), F32),
            pltpu.VMEM((TK, NSA_HEADS * QB), BF16),
        ],
        compiler_params=_params(("arbitrary", "arbitrary")),
    )(proj, proj, proj, kc, vc, proj, tail, qnw, ksnw, kwnw, selmap_t)


OUT_TM = 1024
OUT_TN = 512
MRG_TN = 2 * OUT_TN


def _merge_kernel(ys_ref, yn_ref, gs_ref, gn_ref, ws_ref, wn_ref, o_ref):
    for c in range(MRG_TN // OUT_TN):
        cs = slice(c * OUT_TN, (c + 1) * OUT_TN)
        a = _dot(ys_ref[...], ws_ref[:, cs])
        b = _dot(yn_ref[...], wn_ref[:, cs])
        merged = _sigmoid(gs_ref[:, cs].astype(F32)) * a + _sigmoid(gn_ref[:, cs].astype(F32)) * b
        o_ref[:, cs] = merged.astype(BF16)


def _merge(y_ssd, y_nsa, proj, w_s, w_n):
    m = y_ssd.shape[0]
    nb = D_MODEL // MRG_TN
    return pl.pallas_call(
        _merge_kernel,
        grid=(nb, m // OUT_TM),
        in_specs=[
            pl.BlockSpec((OUT_TM, D_MODEL), lambda j, i: (i, 0)),
            pl.BlockSpec((OUT_TM, D_MODEL), lambda j, i: (i, 0)),
            pl.BlockSpec((OUT_TM, MRG_TN), lambda j, i: (i, C_GLSSD // MRG_TN + j)),
            pl.BlockSpec((OUT_TM, MRG_TN), lambda j, i: (i, C_GLNSA // MRG_TN + j)),
            pl.BlockSpec((D_MODEL, MRG_TN), lambda j, i: (0, j)),
            pl.BlockSpec((D_MODEL, MRG_TN), lambda j, i: (0, j)),
        ],
        out_specs=pl.BlockSpec((OUT_TM, MRG_TN), lambda j, i: (i, j)),
        out_shape=jax.ShapeDtypeStruct((m, D_MODEL), BF16),
        compiler_params=_params(("arbitrary", "arbitrary")),
    )(y_ssd, y_nsa, proj, proj, w_s, w_n)


FIN_TM = 512


def _final_kernel(m_ref, x_ref, w_ref, o_ref):
    for j in range(D_MODEL // OUT_TN):
        cs = slice(j * OUT_TN, (j + 1) * OUT_TN)
        o_ref[:, cs] = x_ref[:, cs] + _dot(m_ref[...], w_ref[:, cs])


def _final(merged, x2, w_o):
    m = merged.shape[0]
    return pl.pallas_call(
        _final_kernel,
        grid=(m // FIN_TM,),
        in_specs=[
            pl.BlockSpec((FIN_TM, D_MODEL), lambda i: (i, 0)),
            pl.BlockSpec((FIN_TM, D_MODEL), lambda i: (i, 0)),
            pl.BlockSpec((D_MODEL, D_MODEL), lambda i: (0, 0)),
        ],
        out_specs=pl.BlockSpec((FIN_TM, D_MODEL), lambda i: (i, 0)),
        out_shape=jax.ShapeDtypeStruct((m, D_MODEL), F32),
        compiler_params=_params(("arbitrary",)),
    )(merged, x2, w_o)


def _w_in_sections(w_in):
    w_b = w_in.astype(BF16)
    sec = {name: w_b[:, _OFFS[i]:_OFFS[i + 1]] for i, name in enumerate(
        ("z_ssd", "xbc", "dt", "q", "k_cmp", "v_cmp", "k_slc", "v_slc", "k_win", "v_win",
         "z_nsa", "gates", "gl_ssd", "gl_nsa"))}
    main = jnp.concatenate([
        sec["z_ssd"], sec["xbc"][:, :SSD_D_INNER], sec["q"], sec["z_nsa"], sec["gl_ssd"], sec["gl_nsa"],
        sec["xbc"][:, SSD_D_INNER:], sec["k_cmp"], sec["v_cmp"], sec["k_slc"], sec["v_slc"],
        sec["k_win"], sec["v_win"], jnp.zeros((D_MODEL, N_PAD - N_MAIN), BF16)], axis=1)
    pad = LANES - SSD_HEADS - NSA_HEADS * N_NSA_BRANCHES
    tail = jnp.concatenate([sec["dt"], sec["gates"], jnp.zeros((D_MODEL, pad), BF16)], axis=1)
    return main, tail


def _constants():
    tri = np.tril(np.ones((CQ, CQ), np.float32))
    expand = np.zeros((LANES, SSD_D_INNER), np.float32)
    for h in range(SSD_HEADS):
        expand[h, h * SSD_HEAD_DIM:(h + 1) * SSD_HEAD_DIM] = 1.0
    n_cmp = N_CMP_PAD - 1
    ci = np.arange(n_cmp)[:, None] * CMP_STRIDE
    sj = np.arange(32)[None, :] * SEL_BLOCK
    sel_map = ((ci < sj + SEL_BLOCK) & (ci + CMP_BLOCK > sj)).astype(np.float32)
    selmap_t = np.zeros((N_SLC_PAD, N_CMP_PAD), np.float32)
    selmap_t[:32, :n_cmp] = sel_map.T
    shift = np.zeros(((SSD_CONV_K - 1) * CQ, CONV_HIST + CQ), np.float32)
    for k in range(SSD_CONV_K - 1):
        for t in range(CQ):
            shift[k * CQ + t, CONV_HIST + t - (SSD_CONV_K - 1) + k] = 1.0
    return jnp.asarray(tri, BF16), jnp.asarray(expand, BF16), jnp.asarray(selmap_t, BF16), jnp.asarray(shift, BF16)


def _pad_lanes(v):
    return jnp.concatenate([v.astype(F32), jnp.zeros((LANES - v.shape[0],), F32)])[None, :]


def kernel(x, norm_w, w_in, conv_w, conv_b, dt_bias, a_log, d_skip, ssd_norm_w, q_norm_w, k_cmp_norm_w,
           k_slc_norm_w, k_win_norm_w, cmp_pe_k, cmp_w1_k, cmp_b1_k, cmp_w2_k, cmp_pe_v, cmp_w1_v, cmp_b1_v,
           cmp_w2_v, w_out_ssd, w_out_nsa, w_o):
    batch, seq, _ = x.shape
    assert seq == 2048 and (batch * seq) % IN_TM == 0
    x2 = x.reshape(batch * seq, D_MODEL)
    w_main, w_tail = _w_in_sections(w_in)
    tri, expand, selmap_t, shift = _constants()

    proj, tail = _inproj(x2, norm_w[None, :], w_main, w_tail)

    y_ssd = _ssd(proj, tail,
                 conv_w[:, :SSD_D_INNER], conv_b[None, :SSD_D_INNER],
                 conv_w[:, SSD_D_INNER:], conv_b[None, SSD_D_INNER:],
                 _pad_lanes(dt_bias), _pad_lanes(a_log),
                 jnp.repeat(d_skip, SSD_HEAD_DIM)[None, :], ssd_norm_w[None, :],
                 tri, expand, shift, batch, seq)

    kc, vc = _compress(proj, cmp_pe_k, cmp_w1_k.astype(BF16), cmp_b1_k[None, :], cmp_w2_k.astype(BF16),
                       cmp_pe_v, cmp_w1_v.astype(BF16), cmp_b1_v[None, :], cmp_w2_v.astype(BF16),
                       k_cmp_norm_w[None, :], batch, seq)

    y_nsa = _nsa(proj, tail, kc, vc, q_norm_w[None, :], k_slc_norm_w[None, :], k_win_norm_w[None, :],
                 selmap_t, batch, seq)

    merged = _merge(y_ssd, y_nsa, proj, w_out_ssd.astype(BF16), w_out_nsa.astype(BF16))
    out = _final(merged, x2, w_o.astype(BF16))
    return out.reshape(batch, seq, D_MODEL)
```
